```python
import math
import jax
import jax.numpy as jnp
from jax import lax
import numpy as np

D_MODEL = 1024
BATCH = 8
SEQ = 8192
DEPTH = 2

GRID_W = 64
HEAD_DIM = 64
N_BRANCH = 4
BRANCH_W = 512
A_HEADS = 4
A_QK_W = A_HEADS * 2 * HEAD_DIM
A_V_W = A_HEADS * 2 * HEAD_DIM
B_PATTERNS = ((128, 1), (512, 4), (2048, 16))
B_HEADS = 8
B_QBLOCK = 64
B_W = B_HEADS * HEAD_DIM
C_HEADS = 8
C_W = C_HEADS * HEAD_DIM
NA_KH = 8
NA_KW = 16
NA_COLBLOCK = 16
NA_COLSPAN = 32
D_HEADS = 8
D_KV_HEADS = 2
D_Q_W = D_HEADS * HEAD_DIM
D_KV_W = D_KV_HEADS * HEAD_DIM
ROPE_THETA = 10000.0
Q_BLOCK = 128
D_FF = 4 * D_MODEL
GATE_W = N_BRANCH * D_MODEL
IN_W = 2 * A_QK_W + A_V_W + 3 * B_W * 3 + 3 * C_W + D_Q_W + 2 * D_KV_W + GATE_W
EPS = 1e-6
NEG_INF = -1e30
F32 = jnp.float32

kernel_name = 'hybrid_gated_multimixer_encoder'


def rms_norm(x, g):
    xf = x.astype(F32)
    y = xf * lax.rsqrt(jnp.mean(xf * xf, axis=-1, keepdims=True) + EPS)
    return (y * g.astype(F32)).astype(x.dtype)


def alibi_slopes(n):
    return jnp.asarray(np.array([2.0 ** (-8.0 * (i + 1) / n) for i in range(n)], dtype=np.float32))


def split_cols(x, sizes):
    points, acc = [], 0
    for s in sizes[:-1]:
        acc += s
        points.append(acc)
    return jnp.split(x, points, axis=-1)


def diff_attention(q, k, v, lam, subln_g, out_scale):
    B_, S, H = q.shape[:3]
    nb = S // Q_BLOCK
    slopes = alibi_slopes(H)
    pos = jnp.arange(S)
    scale = HEAD_DIM ** -0.5
    qb = q.reshape(B_, nb, Q_BLOCK, H, 2, HEAD_DIM).transpose(1, 0, 2, 3, 4, 5)

    def block(args):
        i, qi = args
        tq = i * Q_BLOCK + jnp.arange(Q_BLOCK)
        dist = jnp.abs(tq[:, None] - pos[None, :]).astype(F32)
        bias = -slopes[:, None, None] * dist[None]
        s = jnp.einsum('bqhcd,bshcd->bhcqs', qi, k, preferred_element_type=F32) * scale
        p = jax.nn.softmax(s + bias[None, :, None], axis=-1)
        a = p[:, :, 0] - lam * p[:, :, 1]
        return jnp.einsum('bhqs,bshe->bqhe', a.astype(v.dtype), v)

    o = lax.map(block, (jnp.arange(nb), qb))
    o = o.transpose(1, 0, 2, 3, 4).reshape(B_, S, H, 2 * HEAD_DIM)
    o = rms_norm(o, subln_g) * out_scale
    return o.reshape(B_, S, H * 2 * HEAD_DIM)


def dilated_pattern(q, k, v, dil, radius, slopes):
    B_, S, H, Dh = q.shape
    L = S // dil
    nb = -(-L // B_QBLOCK)
    Lp = nb * B_QBLOCK
    pad = Lp - L

    def to_sub(x):
        return x.reshape(B_, L, dil, H, Dh).transpose(0, 2, 1, 3, 4)

    qs = jnp.pad(to_sub(q), ((0, 0), (0, 0), (0, pad), (0, 0), (0, 0)))
    ks = jnp.pad(to_sub(k), ((0, 0), (0, 0), (radius, pad + radius), (0, 0), (0, 0)))
    vs = jnp.pad(to_sub(v), ((0, 0), (0, 0), (radius, pad + radius), (0, 0), (0, 0)))
    span = B_QBLOCK + 2 * radius
    kidx = jnp.arange(nb)[:, None] * B_QBLOCK + jnp.arange(span)[None, :]
    kb = ks[:, :, kidx]
    vb = vs[:, :, kidx]
    qb = qs.reshape(B_, dil, nb, B_QBLOCK, H, Dh)
    qpos = jnp.arange(Lp).reshape(nb, B_QBLOCK)
    kpos = kidx - radius
    rel = kpos[:, None, :] - qpos[:, :, None]
    valid = (jnp.abs(rel) <= radius) & (kpos[:, None, :] >= 0) & (kpos[:, None, :] < L)
    dist = (jnp.abs(rel) * dil).astype(F32)
    bias = -slopes[None, :, None, None] * dist[:, None]
    s = jnp.einsum('bjnqhd,bjnkhd->bjnhqk', qb, kb, preferred_element_type=F32) * (Dh ** -0.5) + bias
    s = jnp.where(valid[:, None], s, NEG_INF)
    m = jnp.max(s, axis=-1)
    p = jnp.exp(s - m[..., None])
    den = jnp.sum(p, axis=-1)
    o = jnp.einsum('bjnhqk,bjnkhd->bjnqhd', p.astype(v.dtype), vb, preferred_element_type=F32)
    o = o / den.transpose(0, 1, 2, 4, 3)[..., None]
    o = o.reshape(B_, dil, Lp, H, Dh)[:, :, :L].transpose(0, 2, 1, 3, 4).reshape(B_, S, H, Dh)

    def back(t):
        t = t.transpose(0, 1, 2, 4, 3).reshape(B_, dil, Lp, H)[:, :, :L]
        return t.transpose(0, 2, 1, 3).reshape(B_, S, H)

    return o, back(m), back(den)


def dilated_mixture(b_qkv):
    B_, S = b_qkv.shape[:2]
    slopes = alibi_slopes(B_HEADS)
    outs, ms, dens = [], [], []
    for g, (window, dil) in enumerate(B_PATTERNS):
        o, m, den = dilated_pattern(b_qkv[:, :, g, 0], b_qkv[:, :, g, 1], b_qkv[:, :, g, 2],
                                    dil, window // (2 * dil), slopes)
        outs.append(o)
        ms.append(m)
        dens.append(den)
    m_star = jnp.max(jnp.stack(ms), axis=0)
    ws = [d * jnp.exp(m - m_star) for d, m in zip(dens, ms)]
    num = ws[0][..., None] * outs[0]
    tot = ws[0]
    for w, o in zip(ws[1:], outs[1:]):
        num = num + w[..., None] * o
        tot = tot + w
    out = num / tot[..., None]
    return out.astype(b_qkv.dtype).reshape(B_, S, B_W)


def neighbourhood_attention(q, k, v, rpb):
    B_, S, H, Dh = q.shape
    rows = S // GRID_W
    kh = min(NA_KH, rows)
    ncb = GRID_W // NA_COLBLOCK
    K = kh * NA_COLSPAN
    qg = q.reshape(B_, rows, GRID_W, H, Dh)
    kg = k.reshape(B_, rows, GRID_W, H, Dh)
    vg = v.reshape(B_, rows, GRID_W, H, Dh)
    qcol = jnp.arange(GRID_W).reshape(ncb, NA_COLBLOCK)
    qstart = jnp.clip(qcol - NA_KW // 2, 0, GRID_W - NA_KW)
    kstart = jnp.clip(jnp.arange(ncb) * NA_COLBLOCK - NA_KW // 2, 0, GRID_W - NA_COLSPAN)
    kcol = kstart[:, None] + jnp.arange(NA_COLSPAN)[None, :]
    col_ok = (kcol[:, None, :] >= qstart[:, :, None]) & (kcol[:, None, :] < qstart[:, :, None] + NA_KW)
    mask = jnp.broadcast_to(col_ok[:, :, None, :], (ncb, NA_COLBLOCK, kh, NA_COLSPAN)).reshape(ncb, NA_COLBLOCK, K)
    col_idx = jnp.clip(kcol[:, None, :] - qcol[:, :, None] + NA_KW - 1, 0, 2 * NA_KW - 2)

    def row_fn(r):
        rs = jnp.clip(r - kh // 2, 0, rows - kh)
        kr = lax.dynamic_slice_in_dim(kg, rs, kh, axis=1)[:, :, kcol]
        vr = lax.dynamic_slice_in_dim(vg, rs, kh, axis=1)[:, :, kcol]
        kr = kr.transpose(0, 2, 1, 3, 4, 5).reshape(B_, ncb, K, H, Dh)
        vr = vr.transpose(0, 2, 1, 3, 4, 5).reshape(B_, ncb, K, H, Dh)
        qr = lax.dynamic_index_in_dim(qg, r, axis=1, keepdims=False).reshape(B_, ncb, NA_COLBLOCK, H, Dh)
        row_idx = rs + jnp.arange(kh) - r + NA_KH - 1
        bias = rpb[:, row_idx[None, None, :, None], col_idx[:, :, None, :]]
        bias = bias.reshape(H, ncb, NA_COLBLOCK, K).astype(F32)
        s = jnp.einsum('bcqhd,bckhd->bhcqk', qr, kr, preferred_element_type=F32) * (Dh ** -0.5) + bias[None]
        s = jnp.where(mask[None, None], s, NEG_INF)
        p = jax.nn.softmax(s, axis=-1)
        o = jnp.einsum('bhcqk,bckhd->bcqhd', p.astype(vr.dtype), vr)
        return o.reshape(B_, GRID_W, H, Dh)

    o = lax.map(row_fn, jnp.arange(rows))
    return o.transpose(1, 0, 2, 3, 4).reshape(B_, S, H * Dh)


def _rotate(x, pos):
    n = x.shape[-1] // 2
    inv = ROPE_THETA ** (-jnp.arange(n, dtype=F32) / n)
    ang = pos.astype(F32)[:, None] * inv[None, :]
    cos = jnp.cos(ang)[None, :, None, :]
    sin = jnp.sin(ang)[None, :, None, :]
    x1, x2 = x[..., :n], x[..., n:]
    return jnp.concatenate([x1 * cos - x2 * sin, x2 * cos + x1 * sin], axis=-1)


def axial_rope(x, row_pos, col_pos):
    xf = x.astype(F32)
    half = x.shape[-1] // 2
    out = jnp.concatenate([_rotate(xf[..., :half], row_pos), _rotate(xf[..., half:], col_pos)], axis=-1)
    return out.astype(x.dtype)


def gqa_attention(q, k, v):
    B_, S, Hq, Dh = q.shape
    Hkv = k.shape[2]
    rep = Hq // Hkv
    nb = S // Q_BLOCK
    qb = q.reshape(B_, nb, Q_BLOCK, Hkv, rep, Dh).transpose(1, 0, 2, 3, 4, 5)

    def block(qi):
        s = jnp.einsum('bqgrd,bsgd->bgrqs', qi, k, preferred_element_type=F32) * (Dh ** -0.5)
        p = jax.nn.softmax(s, axis=-1)
        return jnp.einsum('bgrqs,bsgd->bqgrd', p.astype(v.dtype), v)

    o = lax.map(block, qb)
    return o.transpose(1, 0, 2, 3, 4, 5).reshape(B_, S, Hq * Dh)


def setup_inputs(seed: int = 0) -> dict:
    key = jax.random.key(seed)
    ks = jax.random.split(key, 14)

    def nrm(k, shape, scale):
        return scale * jax.random.normal(k, shape, F32)

    return {
        'x': nrm(ks[0], (BATCH, SEQ, D_MODEL), 1.0),
        'norm_mix': 1.0 + nrm(ks[1], (DEPTH, D_MODEL), 0.05),
        'w_in': nrm(ks[2], (DEPTH, D_MODEL, IN_W), D_MODEL ** -0.5),
        'b_gate': nrm(ks[3], (DEPTH, GATE_W), 0.1),
        'diff_lambda': nrm(ks[4], (DEPTH, 4, HEAD_DIM), 0.1),
        'diff_subln': 1.0 + nrm(ks[5], (DEPTH, 2 * HEAD_DIM), 0.05),
        'na_rpb': nrm(ks[6], (DEPTH, C_HEADS, 2 * NA_KH - 1, 2 * NA_KW - 1), 0.1),
        'qk_norm': 1.0 + nrm(ks[7], (DEPTH, 2, HEAD_DIM), 0.05),
        'w_branch': nrm(ks[8], (DEPTH, N_BRANCH, BRANCH_W, D_MODEL), BRANCH_W ** -0.5),
        'w_out': nrm(ks[9], (DEPTH, D_MODEL, D_MODEL), D_MODEL ** -0.5),
        'norm_ffn': 1.0 + nrm(ks[10], (DEPTH, D_MODEL), 0.05),
        'w_ff1': nrm(ks[11], (DEPTH, D_MODEL, D_FF), D_MODEL ** -0.5),
        'w_ff2': nrm(ks[12], (DEPTH, D_FF, D_MODEL), D_FF ** -0.5),
        'norm_final': 1.0 + nrm(ks[13], (D_MODEL,), 0.05),
    }


def reference(x, norm_mix, w_in, b_gate, diff_lambda, diff_subln, na_rpb, qk_norm, w_branch, w_out,
              norm_ffn, w_ff1, w_ff2, norm_final):
    B_, S, _ = x.shape
    pos = jnp.arange(S)
    row_pos = pos // GRID_W
    col_pos = pos % GRID_W
    sizes = [A_QK_W, A_QK_W, A_V_W, len(B_PATTERNS) * 3 * B_W, C_W, C_W, C_W, D_Q_W, D_KV_W, D_KV_W, GATE_W]
    for l in range(DEPTH):
        h = rms_norm(x, norm_mix[l])
        proj = jnp.einsum('bsd,de->bse', h, w_in[l])
        a_q, a_k, a_v, b_qkv, c_q, c_k, c_v, d_q, d_k, d_v, gate = split_cols(proj, sizes)

        lamp = diff_lambda[l].astype(F32)
        lam_init = 0.8 - 0.6 * math.exp(-0.3 * l)
        lam = jnp.exp(jnp.sum(lamp[0] * lamp[1])) - jnp.exp(jnp.sum(lamp[2] * lamp[3])) + lam_init
        y_a = diff_attention(a_q.reshape(B_, S, A_HEADS, 2, HEAD_DIM), a_k.reshape(B_, S, A_HEADS, 2, HEAD_DIM),
                             a_v.reshape(B_, S, A_HEADS, 2 * HEAD_DIM), lam, diff_subln[l], 1.0 - lam_init)

        y_b = dilated_mixture(b_qkv.reshape(B_, S, len(B_PATTERNS), 3, B_HEADS, HEAD_DIM))

        y_c = neighbourhood_attention(c_q.reshape(B_, S, C_HEADS, HEAD_DIM), c_k.reshape(B_, S, C_HEADS, HEAD_DIM),
                                      c_v.reshape(B_, S, C_HEADS, HEAD_DIM), na_rpb[l])

        qd = axial_rope(rms_norm(d_q.reshape(B_, S, D_HEADS, HEAD_DIM), qk_norm[l, 0]), row_pos, col_pos)
        kd = axial_rope(rms_norm(d_k.reshape(B_, S, D_KV_HEADS, HEAD_DIM), qk_norm[l, 1]), row_pos, col_pos)
        y_d = gqa_attention(qd, kd, d_v.reshape(B_, S, D_KV_HEADS, HEAD_DIM))

        g = jax.nn.sigmoid((gate + b_gate[l]).astype(F32)).astype(x.dtype).reshape(B_, S, N_BRANCH, D_MODEL)
        branches = (y_a, y_b, y_c, y_d)
        merged = g[:, :, 0] * jnp.einsum('bsk,kd->bsd', branches[0], w_branch[l, 0])
        for n in range(1, N_BRANCH):
            merged = merged + g[:, :, n] * jnp.einsum('bsk,kd->bsd', branches[n], w_branch[l, n])
        x = x + jnp.einsum('bsd,de->bse', merged, w_out[l])

        u = jax.nn.relu(jnp.einsum('bsd,df->bsf', rms_norm(x, norm_ffn[l]), w_ff1[l]))
        x = x + jnp.einsum('bsf,fd->bsd', u * u, w_ff2[l])
    return rms_norm(x, norm_final)
```

```python
import functools
import math

import numpy as np
import jax
import jax.numpy as jnp
from jax import lax
from jax.experimental import pallas as pl
from jax.experimental.pallas import tpu as pltpu

F32 = jnp.float32
BF16 = jnp.bfloat16

D_MODEL = 1024
GRID_W = 64
HEAD_DIM = 64
PAIR_W = 2 * HEAD_DIM
N_BRANCH = 4
BRANCH_W = 512
A_HEADS = 4
B_PATTERNS = ((128, 1), (512, 4), (2048, 16))
B_HEADS = 8
B_RADIUS = 64
C_HEADS = 8
NA_KH = 8
NA_KW = 16
D_HEADS = 8
D_KV_HEADS = 2
ROPE_THETA = 10000.0
D_FF = 4 * D_MODEL
EPS = 1e-6
NEG_INF = -1e30
LOG2E = math.log2(math.e)
QK_SCALE = HEAD_DIM ** -0.5

VMEM_LIMIT_BYTES = 56 * 1024 * 1024


def _cparams(*sem):
    return pltpu.CompilerParams(dimension_semantics=sem, vmem_limit_bytes=VMEM_LIMIT_BYTES)


def _const_spec(shape):
    nd = len(shape)
    return pl.BlockSpec(shape, lambda *_: (0,) * nd)


def _rms(x, g):
    return x * lax.rsqrt(jnp.mean(x * x, axis=-1, keepdims=True) + EPS) * g


LANES = 128


def _normed_rows(x_ref, nw_ref, hs_ref, xs_ref, dil):
    rows, width = x_ref.shape
    sub = rows // dil
    h = _rms(x_ref[...], nw_ref[...])
    if dil == 1:
        hs_ref[...] = h.astype(BF16)
        return
    for c in range(width // LANES):
        xs_ref[c] = h[:, c * LANES:(c + 1) * LANES]
    for j in range(dil):
        for c in range(width // LANES):
            hs_ref[j * sub:(j + 1) * sub, c * LANES:(c + 1) * LANES] = (
                xs_ref[c, pl.ds(j, sub, stride=dil), :].astype(BF16))


def _proj_kernel(x_ref, nw_ref, w_ref, cs_ref, q_ref, k_ref, v_ref, hs_ref, xs_ref, *, dil):
    _normed_rows(x_ref, nw_ref, hs_ref, xs_ref, dil)
    sub = x_ref.shape[0] // dil
    h = hs_ref[...]
    for idx, (o_ref, transposed) in enumerate(((q_ref, True), (k_ref, False), (v_ref, True))):
        lo = idx * BRANCH_W
        acc = jnp.dot(h, w_ref[:, lo:lo + BRANCH_W], preferred_element_type=F32)
        acc = acc * cs_ref[:, lo:lo + BRANCH_W]
        for j in range(dil):
            part = acc[j * sub:(j + 1) * sub, :]
            o_ref[j] = (part.T if transposed else part).astype(o_ref.dtype)


def _project(x, nw, w, colscale, dil):
    B, S, D = x.shape
    L = S // dil
    R = max(1024, 128 * dil)
    R = min(R, S)
    sub = R // dil
    N = w.shape[1]
    t_shape = jax.ShapeDtypeStruct((B, dil, BRANCH_W, L), BF16)
    n_shape = jax.ShapeDtypeStruct((B, dil, L, BRANCH_W), BF16)
    t_spec = pl.BlockSpec((None, dil, BRANCH_W, sub), lambda b, i: (b, 0, 0, i))
    n_spec = pl.BlockSpec((None, dil, sub, BRANCH_W), lambda b, i: (b, 0, i, 0))
    return pl.pallas_call(
        functools.partial(_proj_kernel, dil=dil),
        grid=(B, S // R),
        in_specs=[pl.BlockSpec((None, R, D), lambda b, i: (b, i, 0)),
                  _const_spec((1, D)), _const_spec((D, N)), _const_spec((1, N))],
        out_specs=[t_spec, n_spec, t_spec],
        out_shape=[t_shape, n_shape, t_shape],
        scratch_shapes=[pltpu.VMEM((R, D), BF16),
                        pltpu.VMEM((D // LANES, R if dil > 1 else 8, LANES), F32)],
        compiler_params=_cparams("parallel", "parallel"),
        name=f"proj_dil{dil}",
    )(x, nw, w, colscale)


def _norm_rope_t(yt, g, cos, sg):
    outs = []
    for h in range(yt.shape[0] // HEAD_DIM):
        yh = yt[h * HEAD_DIM:(h + 1) * HEAD_DIM, :]
        yn = yh * lax.rsqrt(jnp.mean(yh * yh, axis=0, keepdims=True) + EPS) * g
        swapped = jnp.concatenate([yn[16:32], yn[0:16], yn[48:64], yn[32:48]], axis=0)
        outs.append(yn * cos + swapped * sg)
    return jnp.concatenate(outs, axis=0)


def _proj_d_kernel(x_ref, nw_ref, w_ref, gq_ref, gk_ref, cos_ref, sg_ref, q_ref, k_ref, v_ref, hs_ref):
    _normed_rows(x_ref, nw_ref, hs_ref, None, 1)
    h = hs_ref[...]
    cos = cos_ref[...]
    sg = sg_ref[...]
    qw = D_HEADS * HEAD_DIM
    kw = D_KV_HEADS * HEAD_DIM
    qt = jnp.dot(h, w_ref[:, 0:qw], preferred_element_type=F32).T
    q_ref[...] = (_norm_rope_t(qt, gq_ref[...], cos, sg) * (QK_SCALE * LOG2E)).astype(BF16)
    kt = jnp.dot(h, w_ref[:, qw:qw + kw], preferred_element_type=F32).T
    k_ref[...] = _norm_rope_t(kt, gk_ref[...], cos, sg).T.astype(BF16)
    v_ref[...] = jnp.dot(h, w_ref[:, qw + kw:qw + 2 * kw], preferred_element_type=F32).T.astype(BF16)


def _project_d(x, nw, w, gq, gk, cos_t, sg_t):
    B, S, D = x.shape
    R = min(1024, S)
    qw = D_HEADS * HEAD_DIM
    kw = D_KV_HEADS * HEAD_DIM
    return pl.pallas_call(
        _proj_d_kernel,
        grid=(B, S // R),
        in_specs=[pl.BlockSpec((None, R, D), lambda b, i: (b, i, 0)),
                  _const_spec((1, D)), _const_spec((D, qw + 2 * kw)),
                  _const_spec((HEAD_DIM, 1)), _const_spec((HEAD_DIM, 1)),
                  pl.BlockSpec((HEAD_DIM, R), lambda b, i: (0, i)),
                  pl.BlockSpec((HEAD_DIM, R), lambda b, i: (0, i))],
        out_specs=[pl.BlockSpec((None, qw, R), lambda b, i: (b, 0, i)),
                   pl.BlockSpec((None, R, kw), lambda b, i: (b, i, 0)),
                   pl.BlockSpec((None, kw, R), lambda b, i: (b, 0, i))],
        out_shape=[jax.ShapeDtypeStruct((B, qw, S), BF16),
                   jax.ShapeDtypeStruct((B, S, kw), BF16),
                   jax.ShapeDtypeStruct((B, kw, S), BF16)],
        scratch_shapes=[pltpu.VMEM((R, D), BF16)],
        compiler_params=_cparams("parallel", "parallel"),
        name="proj_gqa",
    )(x, nw, w, gq, gk, cos_t, sg_t)


FLASH_COLS = 256


def _stack_pair_queries(q2_ref, qt_ref, nchunk, tq):
    row = lax.broadcasted_iota(jnp.int32, (PAIR_W, tq), 0)
    for c in range(nchunk):
        qc = qt_ref[c * PAIR_W:(c + 1) * PAIR_W, :].astype(F32)
        q2_ref[:, (2 * c) * tq:(2 * c + 1) * tq] = jnp.where(row < HEAD_DIM, qc, 0.0).astype(BF16)
        q2_ref[:, (2 * c + 1) * tq:(2 * c + 2) * tq] = jnp.where(row >= HEAD_DIM, qc, 0.0).astype(BF16)


def _flash_loop(k_ref, vt_ref, q2_ref, m_ref, l_ref, a_ref, acc_ref, p_ref, *, tq, tk, seq, bias_fn):
    M = q2_ref.shape[1]
    m_ref[...] = jnp.full(m_ref.shape, NEG_INF, F32)
    l_ref[...] = jnp.zeros(l_ref.shape, F32)
    acc_ref[...] = jnp.zeros(acc_ref.shape, F32)

    def body(j, carry):
        off = pl.multiple_of(j * tk, tk)
        k = k_ref[pl.ds(off, tk), :]
        for c0 in range(0, M, FLASH_COLS):
            cols = slice(c0, c0 + FLASH_COLS)
            s = jnp.dot(k, q2_ref[:, cols], preferred_element_type=F32)
            if bias_fn is not None:
                s = s + bias_fn(j, c0 % tq)
            m_old = m_ref[:, cols]
            m_new = jnp.maximum(m_old, jnp.max(s, axis=0, keepdims=True))
            p = jnp.exp2(s - m_new)
            alpha = jnp.exp2(m_old - m_new)
            l_ref[:, cols] = alpha * l_ref[:, cols] + jnp.sum(p, axis=0, keepdims=True)
            m_ref[:, cols] = m_new
            a_ref[:, cols] = alpha
            p_ref[:, cols] = p.astype(BF16)
        vt = vt_ref[:, pl.ds(off, tk)]
        acc_ref[...] = acc_ref[...] * a_ref[...] + jnp.dot(vt, p_ref[...], preferred_element_type=F32)
        return carry

    lax.fori_loop(0, seq // tk, body, 0)


def _flash_a_kernel(slope_ref, lam_ref, g_ref, qt_ref, k_ref, vt_ref, o_ref,
                    q2_ref, m_ref, l_ref, a_ref, acc_ref, p_ref, *, tq, tk, seq, lam_init):
    _stack_pair_queries(q2_ref, qt_ref, 1, tq)
    negc2 = -LOG2E * slope_ref[pl.program_id(1)]
    q0 = pl.program_id(2) * tq
    d0 = (lax.broadcasted_iota(jnp.int32, (tk, FLASH_COLS), 1)
          - lax.broadcasted_iota(jnp.int32, (tk, FLASH_COLS), 0)).astype(F32)

    def bias_fn(j, qcol0):
        delta = (q0 + qcol0 - j * tk).astype(F32)
        return jnp.abs(d0 + delta) * negc2

    _flash_loop(k_ref, vt_ref, q2_ref, m_ref, l_ref, a_ref, acc_ref, p_ref,
                tq=tq, tk=tk, seq=seq, bias_fn=bias_fn)

    lp = lam_ref[...]
    lam = (jnp.exp(jnp.sum(lp[0:1] * lp[1:2], axis=1, keepdims=True))
           - jnp.exp(jnp.sum(lp[2:3] * lp[3:4], axis=1, keepdims=True)) + lam_init)
    rl = 1.0 / l_ref[...]
    o = acc_ref[:, 0:tq] * rl[:, 0:tq] - lam * (acc_ref[:, tq:2 * tq] * rl[:, tq:2 * tq])
    o = o * lax.rsqrt(jnp.mean(o * o, axis=0, keepdims=True) + EPS) * g_ref[...]
    o_ref[...] = (o * (1.0 - lam_init)).T.astype(o_ref.dtype)


def _flash_a(slopes, lamp, subg, qt, k, vt, lam_init, tq, tk):
    B, _, S = qt.shape
    tq, tk = min(tq, S), min(tk, S)
    M = 2 * tq
    return pl.pallas_call(
        functools.partial(_flash_a_kernel, tq=tq, tk=tk, seq=S, lam_init=lam_init),
        grid=(B, A_HEADS, S // tq),
        in_specs=[pl.BlockSpec(memory_space=pltpu.SMEM),
                  _const_spec((4, HEAD_DIM)), _const_spec((PAIR_W, 1)),
                  pl.BlockSpec((None, PAIR_W, tq), lambda b, h, i: (b, h, i)),
                  pl.BlockSpec((None, S, PAIR_W), lambda b, h, i: (b, 0, h)),
                  pl.BlockSpec((None, PAIR_W, S), lambda b, h, i: (b, h, 0))],
        out_specs=pl.BlockSpec((None, tq, PAIR_W), lambda b, h, i: (b, i, h)),
        out_shape=jax.ShapeDtypeStruct((B, S, A_HEADS * PAIR_W), BF16),
        scratch_shapes=[pltpu.VMEM((PAIR_W, M), BF16), pltpu.VMEM((1, M), F32), pltpu.VMEM((1, M), F32),
                        pltpu.VMEM((1, M), F32), pltpu.VMEM((PAIR_W, M), F32), pltpu.VMEM((tk, M), BF16)],
        compiler_params=_cparams("parallel", "parallel", "parallel"),
        name="flash_diff",
    )(slopes, lamp, subg, qt, k, vt)


def _flash_d_kernel(qt_ref, k_ref, vt_ref, o_ref, q2_ref, m_ref, l_ref, a_ref, acc_ref, p_ref, *, tq, tk, seq):
    nchunk = qt_ref.shape[0] // PAIR_W
    _stack_pair_queries(q2_ref, qt_ref, nchunk, tq)
    _flash_loop(k_ref, vt_ref, q2_ref, m_ref, l_ref, a_ref, acc_ref, p_ref,
                tq=tq, tk=tk, seq=seq, bias_fn=None)
    rl = 1.0 / l_ref[...]
    for c in range(nchunk):
        top = slice((2 * c) * tq, (2 * c + 1) * tq)
        bot = slice((2 * c + 1) * tq, (2 * c + 2) * tq)
        ot = jnp.concatenate([acc_ref[0:HEAD_DIM, top] * rl[:, top],
                              acc_ref[HEAD_DIM:PAIR_W, bot] * rl[:, bot]], axis=0)
        o_ref[:, c * PAIR_W:(c + 1) * PAIR_W] = ot.T.astype(o_ref.dtype)


def _flash_d(qt, k, vt, tq, tk):
    B, qw, S = qt.shape
    tq, tk = min(tq, S), min(tk, S)
    M = 2 * (qw // PAIR_W) * tq
    return pl.pallas_call(
        functools.partial(_flash_d_kernel, tq=tq, tk=tk, seq=S),
        grid=(B, S // tq),
        in_specs=[pl.BlockSpec((None, qw, tq), lambda b, i: (b, 0, i)),
                  pl.BlockSpec((None, S, PAIR_W), lambda b, i: (b, 0, 0)),
                  pl.BlockSpec((None, PAIR_W, S), lambda b, i: (b, 0, 0))],
        out_specs=pl.BlockSpec((None, tq, qw), lambda b, i: (b, i, 0)),
        out_shape=jax.ShapeDtypeStruct((B, S, qw), BF16),
        scratch_shapes=[pltpu.VMEM((PAIR_W, M), BF16), pltpu.VMEM((1, M), F32), pltpu.VMEM((1, M), F32),
                        pltpu.VMEM((1, M), F32), pltpu.VMEM((PAIR_W, M), F32), pltpu.VMEM((tk, M), BF16)],
        compiler_params=_cparams("parallel", "parallel"),
        name="flash_gqa",
    )(qt, k, vt)


WIN_TQ = 128


def _win_kernel(tab_ref, qt_ref, k_ref, vt_ref, bias_ref, o_ref, *lse_refs, width, nblocks):
    row = lax.broadcasted_iota(jnp.int32, (PAIR_W, WIN_TQ), 0)

    def body(i, carry):
        ws = pl.multiple_of(tab_ref[i, 0], 128)
        var = tab_ref[i, 1]
        q0 = pl.multiple_of(i * WIN_TQ, WIN_TQ)
        qc = qt_ref[:, pl.ds(q0, WIN_TQ)].astype(F32)
        q2 = jnp.concatenate([jnp.where(row < HEAD_DIM, qc, 0.0),
                              jnp.where(row >= HEAD_DIM, qc, 0.0)], axis=1).astype(BF16)
        kw = k_ref[pl.ds(ws, width), :]
        s = jnp.dot(kw, q2, preferred_element_type=F32) + bias_ref[var]
        m = jnp.max(s, axis=0, keepdims=True)
        p = jnp.exp2(s - m)
        l = jnp.sum(p, axis=0, keepdims=True)
        vw = vt_ref[:, pl.ds(ws, width)]
        acc = jnp.dot(vw, p.astype(BF16), preferred_element_type=F32)
        rl = 1.0 / l
        ot = jnp.concatenate([acc[0:HEAD_DIM, 0:WIN_TQ] * rl[:, 0:WIN_TQ],
                              acc[HEAD_DIM:PAIR_W, WIN_TQ:] * rl[:, WIN_TQ:]], axis=0)
        o_ref[pl.ds(q0, WIN_TQ), :] = ot.T.astype(o_ref.dtype)
        if lse_refs:
            lse = m + jnp.log2(l)
            lt = jnp.concatenate([jnp.broadcast_to(lse[:, 0:WIN_TQ], (HEAD_DIM, WIN_TQ)),
                                  jnp.broadcast_to(lse[:, WIN_TQ:], (HEAD_DIM, WIN_TQ))], axis=0)
            lse_refs[0][pl.ds(q0, WIN_TQ), :] = lt.T
        return carry

    lax.fori_loop(0, nblocks, body, 0)


def _windowed(tab, qt, k, vt, bias, out_dtype, want_lse):
    N, _, L = qt.shape
    nvar, npair, width, _ = bias.shape
    nblocks = L // WIN_TQ
    o_spec = pl.BlockSpec((None, L, PAIR_W), lambda n, p: (n, 0, p))
    out_shape = [jax.ShapeDtypeStruct((N, L, npair * PAIR_W), out_dtype)]
    out_specs = [o_spec]
    if want_lse:
        out_shape.append(jax.ShapeDtypeStruct((N, L, npair * PAIR_W), F32))
        out_specs.append(o_spec)
    return pl.pallas_call(
        functools.partial(_win_kernel, width=width, nblocks=nblocks),
        grid=(N, npair),
        in_specs=[pl.BlockSpec(memory_space=pltpu.SMEM),
                  pl.BlockSpec((None, PAIR_W, L), lambda n, p: (n, p, 0)),
                  pl.BlockSpec((None, L, PAIR_W), lambda n, p: (n, 0, p)),
                  pl.BlockSpec((None, PAIR_W, L), lambda n, p: (n, p, 0)),
                  pl.BlockSpec((nvar, None, width, 2 * WIN_TQ), lambda n, p: (0, p, 0, 0))],
        out_specs=out_specs,
        out_shape=out_shape,
        compiler_params=_cparams("parallel", "parallel"),
        name=f"windowed_w{width}",
    )(tab, qt, k, vt, bias)


def _bmerge_kernel(o0_ref, l0_ref, o1_ref, l1_ref, o2_ref, l2_ref, y_ref, so1, sl1, so2, sl2, *, dils):
    T, width = y_ref.shape
    nchunk = width // LANES
    for (o_ref, l_ref, so, sl, dil) in ((o1_ref, l1_ref, so1, sl1, dils[1]), (o2_ref, l2_ref, so2, sl2, dils[2])):
        sub = T // dil
        for j in range(dil):
            for c in range(nchunk):
                so[c, pl.ds(j, sub, stride=dil), :] = o_ref[j, :, c * LANES:(c + 1) * LANES]
                sl[c, pl.ds(j, sub, stride=dil), :] = l_ref[j, :, c * LANES:(c + 1) * LANES]
    for c in range(nchunk):
        cols = slice(c * LANES, (c + 1) * LANES)
        o0, e0 = o0_ref[0, :, cols], l0_ref[0, :, cols]
        e1, e2 = sl1[c], sl2[c]
        mx = jnp.maximum(jnp.maximum(e0, e1), e2)
        w0, w1, w2 = jnp.exp2(e0 - mx), jnp.exp2(e1 - mx), jnp.exp2(e2 - mx)
        num = w0 * o0 + w1 * so1[c] + w2 * so2[c]
        y_ref[:, cols] = (num / (w0 + w1 + w2)).astype(y_ref.dtype)


def _bmerge(outs, S):
    B = outs[0][0].shape[0]
    dils = tuple(d for _, d in B_PATTERNS)
    T = min(512, S)
    args, in_specs = [], []
    for (o, e), dil in zip(outs, dils):
        spec = pl.BlockSpec((None, dil, T // dil, BRANCH_W), lambda b, t: (b, 0, t, 0))
        args += [o, e]
        in_specs += [spec, spec]
    return pl.pallas_call(
        functools.partial(_bmerge_kernel, dils=dils),
        grid=(B, S // T),
        in_specs=in_specs,
        out_specs=pl.BlockSpec((None, T, BRANCH_W), lambda b, t: (b, t, 0)),
        out_shape=jax.ShapeDtypeStruct((B, S, BRANCH_W), BF16),
        scratch_shapes=[pltpu.VMEM((BRANCH_W // LANES, T, LANES), F32)] * 4,
        compiler_params=_cparams("parallel", "parallel"),
        name="dilated_merge",
    )(*args)


def _merge_kernel(x_ref, nw_ref, ya_ref, yb_ref, yc_ref, yd_ref, wg_ref, bg_ref, wb_ref, wo_ref, o_ref):
    x = x_ref[...]
    h = _rms(x, nw_ref[...]).astype(BF16)
    merged = None
    for n, y_ref in enumerate((ya_ref, yb_ref, yc_ref, yd_ref)):
        cols = slice(n * D_MODEL, (n + 1) * D_MODEL)
        z = jnp.dot(h, wg_ref[:, cols], preferred_element_type=F32) + bg_ref[:, cols]
        gate = 1.0 / (1.0 + jnp.exp(-z))
        t = gate * jnp.dot(y_ref[...], wb_ref[n], preferred_element_type=F32)
        merged = t if merged is None else merged + t
    o_ref[...] = x + jnp.dot(merged.astype(BF16), wo_ref[...], preferred_element_type=F32)


def _merge(x2d, nw, ys, wg, bg, wb, wo, tm):
    Mrows, D = x2d.shape
    tm = min(tm, Mrows)
    row = lambda w: pl.BlockSpec((tm, w), lambda i: (i, 0))
    return pl.pallas_call(
        _merge_kernel,
        grid=(Mrows // tm,),
        in_specs=[row(D), _const_spec((1, D)), row(BRANCH_W), row(BRANCH_W), row(BRANCH_W), row(BRANCH_W),
                  _const_spec(wg.shape), _const_spec(bg.shape), _const_spec(wb.shape), _const_spec(wo.shape)],
        out_specs=row(D),
        out_shape=jax.ShapeDtypeStruct((Mrows, D), F32),
        compiler_params=_cparams("parallel"),
        name="gated_merge",
    )(x2d, nw, *ys, wg, bg, wb, wo)


FFN_CHUNK = 1024


def _ffn_kernel(x_ref, nw_ref, w1_ref, w2_ref, nf_ref, o_ref, *, final):
    x = x_ref[...]
    h = _rms(x, nw_ref[...]).astype(BF16)
    acc = x
    for c0 in range(0, D_FF, FFN_CHUNK):
        u = jnp.maximum(jnp.dot(h, w1_ref[:, c0:c0 + FFN_CHUNK], preferred_element_type=F32), 0.0)
        acc = acc + jnp.dot((u * u).astype(BF16), w2_ref[c0:c0 + FFN_CHUNK, :], preferred_element_type=F32)
    if final:
        acc = _rms(acc, nf_ref[...])
    o_ref[...] = acc


def _ffn(x2d, nw, w1, w2, nf, final, tm):
    Mrows, D = x2d.shape
    tm = min(tm, Mrows)
    row = pl.BlockSpec((tm, D), lambda i: (i, 0))
    return pl.pallas_call(
        functools.partial(_ffn_kernel, final=final),
        grid=(Mrows // tm,),
        in_specs=[row, _const_spec((1, D)), _const_spec(w1.shape), _const_spec(w2.shape), _const_spec((1, D))],
        out_specs=row,
        out_shape=jax.ShapeDtypeStruct((Mrows, D), F32),
        compiler_params=_cparams("parallel"),
        name="relu2_mlp",
    )(x2d, nw, w1, w2, nf)


def _alibi_slopes(n):
    return np.array([2.0 ** (-8.0 * (i + 1) / n) for i in range(n)], dtype=np.float32)


def _pair_tiles(t):
    nvar, nh, w, tq = t.shape
    t = t.reshape(nvar, nh // 2, 2, w, tq)
    return jnp.concatenate([t[:, :, 0], t[:, :, 1]], axis=-1)


def _dilated_tables(L, dil):
    width = 3 * WIN_TQ
    nb = L // WIN_TQ
    ws = np.clip((np.arange(nb) - 1) * WIN_TQ, 0, L - width)
    var = np.where(np.arange(nb) == 0, 0, np.where(np.arange(nb) == nb - 1, 2, 1))
    tab = np.stack([ws, var], axis=1).astype(np.int32)
    c = np.arange(width)[:, None]
    r = np.arange(WIN_TQ)[None, :]
    slopes = _alibi_slopes(B_HEADS)
    tiles = []
    for shift in (0, WIN_TQ, 2 * WIN_TQ):
        rel = c - shift - r
        valid = np.abs(rel) <= B_RADIUS
        bias = -slopes[:, None, None] * (np.abs(rel) * dil).astype(np.float32)[None] * np.float32(LOG2E)
        tiles.append(np.where(valid[None], bias, np.float32(NEG_INF)))
    return jnp.asarray(tab), _pair_tiles(jnp.asarray(np.stack(tiles).astype(np.float32)))


def _na_tables(S, rpb):
    rows = S // GRID_W
    kh = min(NA_KH, rows)
    wrows = 10
    nb = rows // 2
    r0s = 2 * np.arange(nb)
    wsr = np.clip(r0s - kh // 2, 0, rows - wrows)
    tab = np.stack([wsr * GRID_W, (r0s - wsr) // 2], axis=1).astype(np.int32)
    reps = [0, 2, 4, rows - 4, rows - 2]
    krl = np.arange(wrows * GRID_W) // GRID_W
    kc = np.arange(wrows * GRID_W) % GRID_W
    qrl = np.arange(2 * GRID_W) // GRID_W
    qc = np.arange(2 * GRID_W) % GRID_W
    ridx, cidx, valid = [], [], []
    for r0 in reps:
        w0 = int(np.clip(r0 - kh // 2, 0, rows - wrows))
        r = (r0 + qrl)[None, :]
        rs = np.clip(r - kh // 2, 0, rows - kh)
        kr = (w0 + krl)[:, None]
        qstart = np.clip(qc - NA_KW // 2, 0, GRID_W - NA_KW)[None, :]
        ok = (kr >= rs) & (kr < rs + kh) & (kc[:, None] >= qstart) & (kc[:, None] < qstart + NA_KW)
        ridx.append(np.clip(kr - r + NA_KH - 1, 0, 2 * NA_KH - 2))
        cidx.append(np.clip(kc[:, None] - qc[None, :] + NA_KW - 1, 0, 2 * NA_KW - 2))
        valid.append(ok)
    ridx, cidx, valid = np.stack(ridx), np.stack(cidx), np.stack(valid)
    vals = rpb.astype(F32)[:, ridx, cidx] * LOG2E
    tiles = jnp.where(valid[None], vals, NEG_INF).transpose(1, 0, 2, 3)
    return jnp.asarray(tab), _pair_tiles(tiles)


def _rope_tables(S):
    n = HEAD_DIM // 4
    pos = np.arange(S)
    inv = ROPE_THETA ** (-np.arange(n, dtype=np.float32) / n)
    d = np.arange(HEAD_DIM)
    p = np.where((d < HEAD_DIM // 2)[:, None], (pos // GRID_W)[None, :], (pos % GRID_W)[None, :]).astype(np.float32)
    ang = p * inv[d % n][:, None]
    sign = np.where((d % (2 * n)) < n, -1.0, 1.0).astype(np.float32)[:, None]
    return jnp.asarray(np.cos(ang).astype(np.float32)), jnp.asarray((np.sin(ang) * sign).astype(np.float32))


def kernel(x, norm_mix, w_in, b_gate, diff_lambda, diff_subln, na_rpb, qk_norm, w_branch, w_out, norm_ffn,
           w_ff1, w_ff2, norm_final):
    B, S, D = x.shape
    depth = w_in.shape[0]
    rows = B * S

    ones_col = jnp.ones((1, 3 * BRANCH_W), F32)
    qscale_col = ones_col.at[:, 0:BRANCH_W].set(QK_SCALE * LOG2E)
    a_slopes = jnp.asarray(_alibi_slopes(A_HEADS))
    cos_t, sg_t = _rope_tables(S)
    b_tabs = [_dilated_tables(S // dil, dil) for _, dil in B_PATTERNS]
    d_perm = np.array([(g * 4 + c) * HEAD_DIM + d for c in range(4) for g in range(D_KV_HEADS)
                       for d in range(HEAD_DIM)])

    o_a, o_b, o_c, o_d, o_gate = 0, 1536, 6144, 7680, 8448
    for l in range(depth):
        w = w_in[l].astype(BF16)
        nw = norm_mix[l].reshape(1, D)
        lam_init = 0.8 - 0.6 * math.exp(-0.3 * l)

        aq, ak, av = _project(x, nw, w[:, o_a:o_a + 1536], qscale_col, 1)
        y_a = _flash_a(a_slopes, diff_lambda[l], diff_subln[l].reshape(PAIR_W, 1),
                       aq[:, 0], ak[:, 0], av[:, 0], lam_init, 512, 1024)

        b_outs = []
        for g, (_, dil) in enumerate(B_PATTERNS):
            lo = o_b + g * 1536
            bq, bk, bv = _project(x, nw, w[:, lo:lo + 1536], qscale_col, dil)
            L = S // dil
            tab, bias = b_tabs[g]
            o, e = _windowed(tab, bq.reshape(B * dil, BRANCH_W, L), bk.reshape(B * dil, L, BRANCH_W),
                             bv.reshape(B * dil, BRANCH_W, L), bias, F32, True)
            b_outs.append((o.reshape(B, dil, L, BRANCH_W), e.reshape(B, dil, L, BRANCH_W)))
        y_b = _bmerge(b_outs, S)

        cq, ck, cv = _project(x, nw, w[:, o_c:o_c + 1536], qscale_col, 1)
        c_tab, c_bias = _na_tables(S, na_rpb[l])
        (y_c,) = _windowed(c_tab, cq[:, 0], ck[:, 0], cv[:, 0], c_bias, BF16, False)

        wd = jnp.concatenate([w[:, o_d:o_d + 512][:, d_perm], w[:, o_d + 512:o_gate]], axis=1)
        dq, dk, dv = _project_d(x, nw, wd, qk_norm[l, 0].reshape(HEAD_DIM, 1), qk_norm[l, 1].reshape(HEAD_DIM, 1),
                                cos_t, sg_t)
        y_d = _flash_d(dq, dk, dv, 128, 1024)

        wb = w_branch[l].astype(BF16)
        wb = wb.at[3].set(wb[3][d_perm, :])
        ys = [y.reshape(rows, BRANCH_W) for y in (y_a, y_b, y_c, y_d)]
        x2d = _merge(x.reshape(rows, D), nw, ys, w[:, o_gate:], b_gate[l].reshape(1, -1), wb,
                     w_out[l].astype(BF16), 512)

        x2d = _ffn(x2d, norm_ffn[l].reshape(1, D), w_ff1[l].astype(BF16), w_ff2[l].astype(BF16),
                   norm_final.reshape(1, D), l == depth - 1, 512)
        x = x2d.reshape(B, S, D)
    return x
```

```python
import functools
import math

import numpy as np
import jax
import jax.numpy as jnp
from jax import lax
from jax.experimental import pallas as pl
from jax.experimental.pallas import tpu as pltpu

F32 = jnp.float32
BF16 = jnp.bfloat16

D_MODEL = 1024
GRID_W = 64
HEAD_DIM = 64
PAIR_W = 2 * HEAD_DIM
N_BRANCH = 4
BRANCH_W = 512
A_HEADS = 4
B_PATTERNS = ((128, 1), (512, 4), (2048, 16))
B_HEADS = 8
B_RADIUS = 64
C_HEADS = 8
NA_KH = 8
NA_KW = 16
D_HEADS = 8
D_KV_HEADS = 2
ROPE_THETA = 10000.0
D_FF = 4 * D_MODEL
EPS = 1e-6
NEG_INF = -1e30
LOG2E = math.log2(math.e)
QK_SCALE = HEAD_DIM ** -0.5

VMEM_LIMIT_BYTES = 56 * 1024 * 1024


def _cparams(*sem):
    return pltpu.CompilerParams(dimension_semantics=sem, vmem_limit_bytes=VMEM_LIMIT_BYTES)


def _const_spec(shape):
    nd = len(shape)
    return pl.BlockSpec(shape, lambda *_: (0,) * nd)


def _rms(x, g):
    return x * lax.rsqrt(jnp.mean(x * x, axis=-1, keepdims=True) + EPS) * g


LANES = 128


def _normed_rows(x_ref, nw_ref, hs_ref, xs_ref, dil):
    rows, width = x_ref.shape
    sub = rows // dil
    h = _rms(x_ref[...], nw_ref[...])
    if dil == 1:
        hs_ref[...] = h.astype(BF16)
        return
    for c in range(width // LANES):
        xs_ref[c] = h[:, c * LANES:(c + 1) * LANES]
    for j in range(dil):
        for c in range(width // LANES):
            hs_ref[j * sub:(j + 1) * sub, c * LANES:(c + 1) * LANES] = (
                xs_ref[c, pl.ds(j, sub, stride=dil), :].astype(BF16))


ONES_ROWS = 16
VT_ROWS = PAIR_W + ONES_ROWS


def _proj_kernel(x_ref, nw_ref, w_ref, cs_ref, q_ref, k_ref, v_ref, hs_ref, xs_ref, *, dil, v_ones):
    _normed_rows(x_ref, nw_ref, hs_ref, xs_ref, dil)
    sub = x_ref.shape[0] // dil
    h = hs_ref[...]
    for idx, (o_ref, transposed) in enumerate(((q_ref, True), (k_ref, False), (v_ref, True))):
        lo = idx * BRANCH_W
        acc = jnp.dot(h, w_ref[:, lo:lo + BRANCH_W], preferred_element_type=F32)
        acc = acc * cs_ref[:, lo:lo + BRANCH_W]
        for j in range(dil):
            part = acc[j * sub:(j + 1) * sub, :]
            if o_ref is v_ref and v_ones:
                pt = part.T.astype(BF16)
                for p in range(BRANCH_W // PAIR_W):
                    o_ref[j, p * VT_ROWS:p * VT_ROWS + PAIR_W, :] = pt[p * PAIR_W:(p + 1) * PAIR_W, :]
                    o_ref[j, p * VT_ROWS + PAIR_W:(p + 1) * VT_ROWS, :] = jnp.ones((ONES_ROWS, sub), BF16)
            else:
                o_ref[j] = (part.T if transposed else part).astype(o_ref.dtype)


def _project(x, nw, w, colscale, dil, v_ones=False):
    B, S, D = x.shape
    L = S // dil
    R = max(1024, 128 * dil)
    R = min(R, S)
    sub = R // dil
    N = w.shape[1]
    vrows = (BRANCH_W // PAIR_W) * VT_ROWS if v_ones else BRANCH_W
    t_shape = jax.ShapeDtypeStruct((B, dil, BRANCH_W, L), BF16)
    n_shape = jax.ShapeDtypeStruct((B, dil, L, BRANCH_W), BF16)
    v_shape = jax.ShapeDtypeStruct((B, dil, vrows, L), BF16)
    t_spec = pl.BlockSpec((None, dil, BRANCH_W, sub), lambda b, i: (b, 0, 0, i))
    n_spec = pl.BlockSpec((None, dil, sub, BRANCH_W), lambda b, i: (b, 0, i, 0))
    v_spec = pl.BlockSpec((None, dil, vrows, sub), lambda b, i: (b, 0, 0, i))
    return pl.pallas_call(
        functools.partial(_proj_kernel, dil=dil, v_ones=v_ones),
        grid=(B, S // R),
        in_specs=[pl.BlockSpec((None, R, D), lambda b, i: (b, i, 0)),
                  _const_spec((1, D)), _const_spec((D, N)), _const_spec((1, N))],
        out_specs=[t_spec, n_spec, v_spec],
        out_shape=[t_shape, n_shape, v_shape],
        scratch_shapes=[pltpu.VMEM((R, D), BF16),
                        pltpu.VMEM((D // LANES, R if dil > 1 else 8, LANES), F32)],
        compiler_params=_cparams("parallel", "parallel"),
        name=f"proj_dil{dil}",
    )(x, nw, w, colscale)


def _norm_rope_t(yt, g, cos, sg):
    outs = []
    for h in range(yt.shape[0] // HEAD_DIM):
        yh = yt[h * HEAD_DIM:(h + 1) * HEAD_DIM, :]
        yn = yh * lax.rsqrt(jnp.mean(yh * yh, axis=0, keepdims=True) + EPS) * g
        swapped = jnp.concatenate([yn[16:32], yn[0:16], yn[48:64], yn[32:48]], axis=0)
        outs.append(yn * cos + swapped * sg)
    return jnp.concatenate(outs, axis=0)


def _proj_d_kernel(x_ref, nw_ref, w_ref, gq_ref, gk_ref, cos_ref, sg_ref, q_ref, k_ref, v_ref, hs_ref):
    _normed_rows(x_ref, nw_ref, hs_ref, None, 1)
    h = hs_ref[...]
    cos = cos_ref[...]
    sg = sg_ref[...]
    qw = D_HEADS * HEAD_DIM
    kw = D_KV_HEADS * HEAD_DIM
    qt = jnp.dot(h, w_ref[:, 0:qw], preferred_element_type=F32).T
    q_ref[...] = (_norm_rope_t(qt, gq_ref[...], cos, sg) * (QK_SCALE * LOG2E)).astype(BF16)
    kt = jnp.dot(h, w_ref[:, qw:qw + kw], preferred_element_type=F32).T
    k_ref[...] = _norm_rope_t(kt, gk_ref[...], cos, sg).T.astype(BF16)
    v_ref[0:kw, :] = jnp.dot(h, w_ref[:, qw + kw:qw + 2 * kw], preferred_element_type=F32).T.astype(BF16)
    v_ref[kw:, :] = jnp.ones((ONES_ROWS, v_ref.shape[1]), BF16)


def _project_d(x, nw, w, gq, gk, cos_t, sg_t):
    B, S, D = x.shape
    R = min(1024, S)
    qw = D_HEADS * HEAD_DIM
    kw = D_KV_HEADS * HEAD_DIM
    return pl.pallas_call(
        _proj_d_kernel,
        grid=(B, S // R),
        in_specs=[pl.BlockSpec((None, R, D), lambda b, i: (b, i, 0)),
                  _const_spec((1, D)), _const_spec((D, qw + 2 * kw)),
                  _const_spec((HEAD_DIM, 1)), _const_spec((HEAD_DIM, 1)),
                  pl.BlockSpec((HEAD_DIM, R), lambda b, i: (0, i)),
                  pl.BlockSpec((HEAD_DIM, R), lambda b, i: (0, i))],
        out_specs=[pl.BlockSpec((None, qw, R), lambda b, i: (b, 0, i)),
                   pl.BlockSpec((None, R, kw), lambda b, i: (b, i, 0)),
                   pl.BlockSpec((None, VT_ROWS, R), lambda b, i: (b, 0, i))],
        out_shape=[jax.ShapeDtypeStruct((B, qw, S), BF16),
                   jax.ShapeDtypeStruct((B, S, kw), BF16),
                   jax.ShapeDtypeStruct((B, VT_ROWS, S), BF16)],
        scratch_shapes=[pltpu.VMEM((R, D), BF16)],
        compiler_params=_cparams("parallel", "parallel"),
        name="proj_gqa",
    )(x, nw, w, gq, gk, cos_t, sg_t)


FLASH_COLS = 256


def _stack_pair_queries(q2_ref, qt_ref, nchunk, tq):
    row = lax.broadcasted_iota(jnp.int32, (PAIR_W, tq), 0)
    for c in range(nchunk):
        qc = qt_ref[c * PAIR_W:(c + 1) * PAIR_W, :].astype(F32)
        q2_ref[:, (2 * c) * tq:(2 * c + 1) * tq] = jnp.where(row < HEAD_DIM, qc, 0.0).astype(BF16)
        q2_ref[:, (2 * c + 1) * tq:(2 * c + 2) * tq] = jnp.where(row >= HEAD_DIM, qc, 0.0).astype(BF16)


def _flash_scratch(M, tk):
    return [pltpu.VMEM((PAIR_W, M), BF16), pltpu.VMEM((1, M), F32),
            pltpu.VMEM((VT_ROWS, M), F32), pltpu.VMEM((tk, M), BF16),
            pltpu.VMEM((tk, M), F32), pltpu.VMEM((tk, M), F32),
            pltpu.VMEM((1, M), F32), pltpu.VMEM((1, M), F32)]


def _flash_loop(k_ref, vt_ref, q2_ref, m_ref, acc_ref, p_ref, sa_ref, sb_ref, ma_ref, mb_ref,
                *, tq, tk, seq, bias_fn):
    M = q2_ref.shape[1]
    nk = seq // tk
    m_ref[...] = jnp.full(m_ref.shape, NEG_INF, F32)
    acc_ref[...] = jnp.zeros(acc_ref.shape, F32)

    def scores(j, cols, s_ref, mx_ref):
        k = k_ref[pl.ds(pl.multiple_of(j * tk, tk), tk), :]
        s = jnp.dot(k, q2_ref[:, cols], preferred_element_type=F32)
        if bias_fn is not None:
            s = s + bias_fn(j, cols.start % tq)
        s_ref[:, cols] = s
        mx_ref[:, cols] = jnp.max(s, axis=0, keepdims=True)

    def attend(j, cols, s_ref, mx_ref):
        m_old = m_ref[:, cols]
        m_new = jnp.maximum(m_old, mx_ref[:, cols])
        m_ref[:, cols] = m_new
        p_ref[:, cols] = jnp.exp2((s_ref[:, cols] - m_new).astype(BF16))
        vt = vt_ref[:, pl.ds(pl.multiple_of(j * tk, tk), tk)]
        acc_ref[:, cols] = (acc_ref[:, cols] * jnp.exp2(m_old - m_new)
                            + jnp.dot(vt, p_ref[:, cols], preferred_element_type=F32))

    def step(j, cur, nxt):
        for c0 in range(0, M, FLASH_COLS):
            cols = slice(c0, c0 + FLASH_COLS)
            if nxt is not None:
                scores(j + 1, cols, *nxt)
            attend(j, cols, *cur)

    buf_a, buf_b = (sa_ref, ma_ref), (sb_ref, mb_ref)
    for c0 in range(0, M, FLASH_COLS):
        scores(0, slice(c0, c0 + FLASH_COLS), *buf_a)

    def body(i, carry):
        step(2 * i, buf_a, buf_b)
        step(2 * i + 1, buf_b, buf_a)
        return carry

    lax.fori_loop(0, nk // 2 - 1, body, 0)
    step(nk - 2, buf_a, buf_b)
    step(nk - 1, buf_b, None)


def _flash_a_kernel(slope_ref, lam_ref, g_ref, qt_ref, k_ref, vt_ref, o_ref,
                    q2_ref, m_ref, acc_ref, *bufs, tq, tk, seq, lam_init):
    _stack_pair_queries(q2_ref, qt_ref, 1, tq)
    negc2 = -LOG2E * slope_ref[pl.program_id(1)]
    q0 = pl.program_id(2) * tq
    d0 = (lax.broadcasted_iota(jnp.int32, (tk, FLASH_COLS), 1)
          - lax.broadcasted_iota(jnp.int32, (tk, FLASH_COLS), 0)).astype(F32)

    def bias_fn(j, qcol0):
        delta = (q0 + qcol0 - j * tk).astype(F32)
        return jnp.abs(d0 + delta) * negc2

    _flash_loop(k_ref, vt_ref, q2_ref, m_ref, acc_ref, *bufs,
                tq=tq, tk=tk, seq=seq, bias_fn=bias_fn)

    lp = lam_ref[...]
    lam = (jnp.exp(jnp.sum(lp[0:1] * lp[1:2], axis=1, keepdims=True))
           - jnp.exp(jnp.sum(lp[2:3] * lp[3:4], axis=1, keepdims=True)) + lam_init)
    rl = 1.0 / acc_ref[PAIR_W:PAIR_W + 1, :]
    o = (acc_ref[0:PAIR_W, 0:tq] * rl[:, 0:tq]
         - lam * (acc_ref[0:PAIR_W, tq:2 * tq] * rl[:, tq:2 * tq]))
    o = o * lax.rsqrt(jnp.mean(o * o, axis=0, keepdims=True) + EPS) * g_ref[...]
    o_ref[...] = (o * (1.0 - lam_init)).T.astype(o_ref.dtype)


def _flash_a(slopes, lamp, subg, qt, k, vt, lam_init, tq, tk):
    B, _, S = qt.shape
    tq, tk = min(tq, S), min(tk, S)
    M = 2 * tq
    return pl.pallas_call(
        functools.partial(_flash_a_kernel, tq=tq, tk=tk, seq=S, lam_init=lam_init),
        grid=(B, A_HEADS, S // tq),
        in_specs=[pl.BlockSpec(memory_space=pltpu.SMEM),
                  _const_spec((4, HEAD_DIM)), _const_spec((PAIR_W, 1)),
                  pl.BlockSpec((None, PAIR_W, tq), lambda b, h, i: (b, h, i)),
                  pl.BlockSpec((None, S, PAIR_W), lambda b, h, i: (b, 0, h)),
                  pl.BlockSpec((None, VT_ROWS, S), lambda b, h, i: (b, h, 0))],
        out_specs=pl.BlockSpec((None, tq, PAIR_W), lambda b, h, i: (b, i, h)),
        out_shape=jax.ShapeDtypeStruct((B, S, A_HEADS * PAIR_W), BF16),
        scratch_shapes=_flash_scratch(M, tk),
        compiler_params=_cparams("parallel", "parallel", "parallel"),
        name="flash_diff",
    )(slopes, lamp, subg, qt, k, vt)


def _flash_d_kernel(qt_ref, k_ref, vt_ref, o_ref, q2_ref, m_ref, acc_ref, *bufs, tq, tk, seq):
    nchunk = qt_ref.shape[0] // PAIR_W
    _stack_pair_queries(q2_ref, qt_ref, nchunk, tq)
    _flash_loop(k_ref, vt_ref, q2_ref, m_ref, acc_ref, *bufs,
                tq=tq, tk=tk, seq=seq, bias_fn=None)
    rl = 1.0 / acc_ref[PAIR_W:PAIR_W + 1, :]
    for c in range(nchunk):
        top = slice((2 * c) * tq, (2 * c + 1) * tq)
        bot = slice((2 * c + 1) * tq, (2 * c + 2) * tq)
        ot = jnp.concatenate([acc_ref[0:HEAD_DIM, top] * rl[:, top],
                              acc_ref[HEAD_DIM:PAIR_W, bot] * rl[:, bot]], axis=0)
        o_ref[:, c * PAIR_W:(c + 1) * PAIR_W] = ot.T.astype(o_ref.dtype)


def _flash_d(qt, k, vt, tq, tk):
    B, qw, S = qt.shape
    tq, tk = min(tq, S), min(tk, S)
    M = 2 * (qw // PAIR_W) * tq
    return pl.pallas_call(
        functools.partial(_flash_d_kernel, tq=tq, tk=tk, seq=S),
        grid=(B, S // tq),
        in_specs=[pl.BlockSpec((None, qw, tq), lambda b, i: (b, 0, i)),
                  pl.BlockSpec((None, S, PAIR_W), lambda b, i: (b, 0, 0)),
                  pl.BlockSpec((None, VT_ROWS, S), lambda b, i: (b, 0, 0))],
        out_specs=pl.BlockSpec((None, tq, qw), lambda b, i: (b, i, 0)),
        out_shape=jax.ShapeDtypeStruct((B, S, qw), BF16),
        scratch_shapes=_flash_scratch(M, tk),
        compiler_params=_cparams("parallel", "parallel"),
        name="flash_gqa",
    )(qt, k, vt)


WIN_TQ = 128


def _win_kernel(tab_ref, qt_ref, k_ref, vt_ref, bias_ref, o_ref, *lse_refs, width, nblocks):
    row = lax.broadcasted_iota(jnp.int32, (PAIR_W, WIN_TQ), 0)

    def body(i, carry):
        ws = pl.multiple_of(tab_ref[i, 0], 128)
        var = tab_ref[i, 1]
        q0 = pl.multiple_of(i * WIN_TQ, WIN_TQ)
        qc = qt_ref[:, pl.ds(q0, WIN_TQ)].astype(F32)
        q2 = jnp.concatenate([jnp.where(row < HEAD_DIM, qc, 0.0),
                              jnp.where(row >= HEAD_DIM, qc, 0.0)], axis=1).astype(BF16)
        kw = k_ref[pl.ds(ws, width), :]
        s = jnp.dot(kw, q2, preferred_element_type=F32) + bias_ref[var]
        m = jnp.max(s, axis=0, keepdims=True)
        p = jnp.exp2(s - m)
        l = jnp.sum(p, axis=0, keepdims=True)
        vw = vt_ref[:, pl.ds(ws, width)]
        acc = jnp.dot(vw, p.astype(BF16), preferred_element_type=F32)
        rl = 1.0 / l
        ot = jnp.concatenate([acc[0:HEAD_DIM, 0:WIN_TQ] * rl[:, 0:WIN_TQ],
                              acc[HEAD_DIM:PAIR_W, WIN_TQ:] * rl[:, WIN_TQ:]], axis=0)
        o_ref[pl.ds(q0, WIN_TQ), :] = ot.T.astype(o_ref.dtype)
        if lse_refs:
            lse = m + jnp.log2(l)
            lt = jnp.concatenate([jnp.broadcast_to(lse[:, 0:WIN_TQ], (HEAD_DIM, WIN_TQ)),
                                  jnp.broadcast_to(lse[:, WIN_TQ:], (HEAD_DIM, WIN_TQ))], axis=0)
            lse_refs[0][pl.ds(q0, WIN_TQ), :] = lt.T
        return carry

    lax.fori_loop(0, nblocks, body, 0)


def _windowed(tab, qt, k, vt, bias, out_dtype, want_lse):
    N, _, L = qt.shape
    nvar, npair, width, _ = bias.shape
    nblocks = L // WIN_TQ
    o_spec = pl.BlockSpec((None, L, PAIR_W), lambda n, p: (n, 0, p))
    out_shape = [jax.ShapeDtypeStruct((N, L, npair * PAIR_W), out_dtype)]
    out_specs = [o_spec]
    if want_lse:
        out_shape.append(jax.ShapeDtypeStruct((N, L, npair * PAIR_W), F32))
        out_specs.append(o_spec)
    return pl.pallas_call(
        functools.partial(_win_kernel, width=width, nblocks=nblocks),
        grid=(N, npair),
        in_specs=[pl.BlockSpec(memory_space=pltpu.SMEM),
                  pl.BlockSpec((None, PAIR_W, L), lambda n, p: (n, p, 0)),
                  pl.BlockSpec((None, L, PAIR_W), lambda n, p: (n, 0, p)),
                  pl.BlockSpec((None, PAIR_W, L), lambda n, p: (n, p, 0)),
                  pl.BlockSpec((nvar, None, width, 2 * WIN_TQ), lambda n, p: (0, p, 0, 0))],
        out_specs=out_specs,
        out_shape=out_shape,
        compiler_params=_cparams("parallel", "parallel"),
        name=f"windowed_w{width}",
    )(tab, qt, k, vt, bias)


def _bmerge_kernel(o0_ref, l0_ref, o1_ref, l1_ref, o2_ref, l2_ref, y_ref, so1, sl1, so2, sl2, *, dils):
    T, width = y_ref.shape
    nchunk = width // LANES
    for (o_ref, l_ref, so, sl, dil) in ((o1_ref, l1_ref, so1, sl1, dils[1]), (o2_ref, l2_ref, so2, sl2, dils[2])):
        sub = T // dil
        for j in range(dil):
            for c in range(nchunk):
                so[c, pl.ds(j, sub, stride=dil), :] = o_ref[j, :, c * LANES:(c + 1) * LANES]
                sl[c, pl.ds(j, sub, stride=dil), :] = l_ref[j, :, c * LANES:(c + 1) * LANES]
    for c in range(nchunk):
        cols = slice(c * LANES, (c + 1) * LANES)
        o0, e0 = o0_ref[0, :, cols], l0_ref[0, :, cols]
        e1, e2 = sl1[c], sl2[c]
        mx = jnp.maximum(jnp.maximum(e0, e1), e2)
        w0, w1, w2 = jnp.exp2(e0 - mx), jnp.exp2(e1 - mx), jnp.exp2(e2 - mx)
        num = w0 * o0 + w1 * so1[c] + w2 * so2[c]
        y_ref[:, cols] = (num / (w0 + w1 + w2)).astype(y_ref.dtype)


def _bmerge(outs, S):
    B = outs[0][0].shape[0]
    dils = tuple(d for _, d in B_PATTERNS)
    T = min(512, S)
    args, in_specs = [], []
    for (o, e), dil in zip(outs, dils):
        spec = pl.BlockSpec((None, dil, T // dil, BRANCH_W), lambda b, t: (b, 0, t, 0))
        args += [o, e]
        in_specs += [spec, spec]
    return pl.pallas_call(
        functools.partial(_bmerge_kernel, dils=dils),
        grid=(B, S // T),
        in_specs=in_specs,
        out_specs=pl.BlockSpec((None, T, BRANCH_W), lambda b, t: (b, t, 0)),
        out_shape=jax.ShapeDtypeStruct((B, S, BRANCH_W), BF16),
        scratch_shapes=[pltpu.VMEM((BRANCH_W // LANES, T, LANES), F32)] * 4,
        compiler_params=_cparams("parallel", "parallel"),
        name="dilated_merge",
    )(*args)


def _merge_kernel(x_ref, nw_ref, ya_ref, yb_ref, yc_ref, yd_ref, wg_ref, bg_ref, wb_ref, wo_ref, o_ref):
    x = x_ref[...]
    h = _rms(x, nw_ref[...]).astype(BF16)
    merged = None
    for n, y_ref in enumerate((ya_ref, yb_ref, yc_ref, yd_ref)):
        cols = slice(n * D_MODEL, (n + 1) * D_MODEL)
        z = jnp.dot(h, wg_ref[:, cols], preferred_element_type=F32) + bg_ref[:, cols]
        gate = 1.0 / (1.0 + jnp.exp(-z))
        t = gate * jnp.dot(y_ref[...], wb_ref[n], preferred_element_type=F32)
        merged = t if merged is None else merged + t
    o_ref[...] = x + jnp.dot(merged.astype(BF16), wo_ref[...], preferred_element_type=F32)


def _merge(x2d, nw, ys, wg, bg, wb, wo, tm):
    Mrows, D = x2d.shape
    tm = min(tm, Mrows)
    row = lambda w: pl.BlockSpec((tm, w), lambda i: (i, 0))
    return pl.pallas_call(
        _merge_kernel,
        grid=(Mrows // tm,),
        in_specs=[row(D), _const_spec((1, D)), row(BRANCH_W), row(BRANCH_W), row(BRANCH_W), row(BRANCH_W),
                  _const_spec(wg.shape), _const_spec(bg.shape), _const_spec(wb.shape), _const_spec(wo.shape)],
        out_specs=row(D),
        out_shape=jax.ShapeDtypeStruct((Mrows, D), F32),
        compiler_params=_cparams("parallel"),
        name="gated_merge",
    )(x2d, nw, *ys, wg, bg, wb, wo)


FFN_CHUNK = 1024


def _ffn_kernel(x_ref, nw_ref, w1_ref, w2_ref, nf_ref, o_ref, *, final):
    x = x_ref[...]
    h = _rms(x, nw_ref[...]).astype(BF16)
    acc = x
    for c0 in range(0, D_FF, FFN_CHUNK):
        u = jnp.maximum(jnp.dot(h, w1_ref[:, c0:c0 + FFN_CHUNK], preferred_element_type=F32), 0.0)
        acc = acc + jnp.dot((u * u).astype(BF16), w2_ref[c0:c0 + FFN_CHUNK, :], preferred_element_type=F32)
    if final:
        acc = _rms(acc, nf_ref[...])
    o_ref[...] = acc


def _ffn(x2d, nw, w1, w2, nf, final, tm):
    Mrows, D = x2d.shape
    tm = min(tm, Mrows)
    row = pl.BlockSpec((tm, D), lambda i: (i, 0))
    return pl.pallas_call(
        functools.partial(_ffn_kernel, final=final),
        grid=(Mrows // tm,),
        in_specs=[row, _const_spec((1, D)), _const_spec(w1.shape), _const_spec(w2.shape), _const_spec((1, D))],
        out_specs=row,
        out_shape=jax.ShapeDtypeStruct((Mrows, D), F32),
        compiler_params=_cparams("parallel"),
        name="relu2_mlp",
    )(x2d, nw, w1, w2, nf)


def _alibi_slopes(n):
    return np.array([2.0 ** (-8.0 * (i + 1) / n) for i in range(n)], dtype=np.float32)


def _pair_tiles(t):
    nvar, nh, w, tq = t.shape
    t = t.reshape(nvar, nh // 2, 2, w, tq)
    return jnp.concatenate([t[:, :, 0], t[:, :, 1]], axis=-1)


def _dilated_tables(L, dil):
    width = 3 * WIN_TQ
    nb = L // WIN_TQ
    ws = np.clip((np.arange(nb) - 1) * WIN_TQ, 0, L - width)
    var = np.where(np.arange(nb) == 0, 0, np.where(np.arange(nb) == nb - 1, 2, 1))
    tab = np.stack([ws, var], axis=1).astype(np.int32)
    c = np.arange(width)[:, None]
    r = np.arange(WIN_TQ)[None, :]
    slopes = _alibi_slopes(B_HEADS)
    tiles = []
    for shift in (0, WIN_TQ, 2 * WIN_TQ):
        rel = c - shift - r
        valid = np.abs(rel) <= B_RADIUS
        bias = -slopes[:, None, None] * (np.abs(rel) * dil).astype(np.float32)[None] * np.float32(LOG2E)
        tiles.append(np.where(valid[None], bias, np.float32(NEG_INF)))
    return jnp.asarray(tab), _pair_tiles(jnp.asarray(np.stack(tiles).astype(np.float32)))


def _na_tables(S, rpb):
    rows = S // GRID_W
    kh = min(NA_KH, rows)
    wrows = 10
    nb = rows // 2
    r0s = 2 * np.arange(nb)
    wsr = np.clip(r0s - kh // 2, 0, rows - wrows)
    tab = np.stack([wsr * GRID_W, (r0s - wsr) // 2], axis=1).astype(np.int32)
    reps = [0, 2, 4, rows - 4, rows - 2]
    krl = np.arange(wrows)
    kc = np.arange(GRID_W)
    qrl = np.arange(2)
    qc = np.arange(GRID_W)
    qstart = np.clip(qc - NA_KW // 2, 0, GRID_W - NA_KW)
    col_ok = (kc[:, None] >= qstart[None, :]) & (kc[:, None] < qstart[None, :] + NA_KW)
    ridx, valid = [], []
    for r0 in reps:
        w0 = int(np.clip(r0 - kh // 2, 0, rows - wrows))
        r = (r0 + qrl)[None, :]
        rs = np.clip(r - kh // 2, 0, rows - kh)
        kr = (w0 + krl)[:, None]
        row_ok = (kr >= rs) & (kr < rs + kh)
        ridx.append(np.clip(kr - r + NA_KH - 1, 0, 2 * NA_KH - 2))
        valid.append(row_ok[:, None, :, None] & col_ok[None, :, None, :])
    ridx, valid = np.stack(ridx), np.stack(valid)
    nh = rpb.shape[0]
    span = GRID_W - NA_KW
    padded = jnp.pad(rpb.astype(F32), ((0, 0), (0, 0), (span, span)), mode="edge")
    by_col = jnp.stack([padded[:, :, GRID_W - 1 - c:2 * GRID_W - 1 - c] for c in range(GRID_W)], axis=-1)
    slabs = jnp.take(by_col, jnp.asarray(ridx.reshape(-1)), axis=1)
    slabs = slabs.reshape(nh, len(reps), wrows, 2, GRID_W, GRID_W).transpose(1, 0, 2, 4, 3, 5)
    tiles = jnp.where(valid[:, None], slabs * LOG2E, NEG_INF)
    tiles = tiles.reshape(len(reps), nh, wrows * GRID_W, 2 * GRID_W)
    return jnp.asarray(tab), _pair_tiles(tiles)


def _rope_tables(S):
    n = HEAD_DIM // 4
    pos = np.arange(S)
    inv = ROPE_THETA ** (-np.arange(n, dtype=np.float32) / n)
    d = np.arange(HEAD_DIM)
    p = np.where((d < HEAD_DIM // 2)[:, None], (pos // GRID_W)[None, :], (pos % GRID_W)[None, :]).astype(np.float32)
    ang = p * inv[d % n][:, None]
    sign = np.where((d % (2 * n)) < n, -1.0, 1.0).astype(np.float32)[:, None]
    return jnp.asarray(np.cos(ang).astype(np.float32)), jnp.asarray((np.sin(ang) * sign).astype(np.float32))


def kernel(x, norm_mix, w_in, b_gate, diff_lambda, diff_subln, na_rpb, qk_norm, w_branch, w_out, norm_ffn,
           w_ff1, w_ff2, norm_final):
    B, S, D = x.shape
    depth = w_in.shape[0]
    rows = B * S

    ones_col = jnp.ones((1, 3 * BRANCH_W), F32)
    qscale_col = ones_col.at[:, 0:BRANCH_W].set(QK_SCALE * LOG2E)
    a_slopes = jnp.asarray(_alibi_slopes(A_HEADS))
    cos_t, sg_t = _rope_tables(S)
    b_tabs = [_dilated_tables(S // dil, dil) for _, dil in B_PATTERNS]
    d_perm = np.array([(g * 4 + c) * HEAD_DIM + d for c in range(4) for g in range(D_KV_HEADS)
                       for d in range(HEAD_DIM)])

    o_a, o_b, o_c, o_d, o_gate = 0, 1536, 6144, 7680, 8448
    for l in range(depth):
        w = w_in[l].astype(BF16)
        nw = norm_mix[l].reshape(1, D)
        lam_init = 0.8 - 0.6 * math.exp(-0.3 * l)

        aq, ak, av = _project(x, nw, w[:, o_a:o_a + 1536], qscale_col, 1, v_ones=True)
        y_a = _flash_a(a_slopes, diff_lambda[l], diff_subln[l].reshape(PAIR_W, 1),
                       aq[:, 0], ak[:, 0], av[:, 0], lam_init, 512, 512)

        b_outs = []
        for g, (_, dil) in enumerate(B_PATTERNS):
            lo = o_b + g * 1536
            bq, bk, bv = _project(x, nw, w[:, lo:lo + 1536], qscale_col, dil)
            L = S // dil
            tab, bias = b_tabs[g]
            o, e = _windowed(tab, bq.reshape(B * dil, BRANCH_W, L), bk.reshape(B * dil, L, BRANCH_W),
                             bv.reshape(B * dil, BRANCH_W, L), bias, F32, True)
            b_outs.append((o.reshape(B, dil, L, BRANCH_W), e.reshape(B, dil, L, BRANCH_W)))
        y_b = _bmerge(b_outs, S)

        cq, ck, cv = _project(x, nw, w[:, o_c:o_c + 1536], qscale_col, 1)
        c_tab, c_bias = _na_tables(S, na_rpb[l])
        (y_c,) = _windowed(c_tab, cq[:, 0], ck[:, 0], cv[:, 0], c_bias, BF16, False)

        wd = jnp.concatenate([w[:, o_d:o_d + 512][:, d_perm], w[:, o_d + 512:o_gate]], axis=1)
        dq, dk, dv = _project_d(x, nw, wd, qk_norm[l, 0].reshape(HEAD_DIM, 1), qk_norm[l, 1].reshape(HEAD_DIM, 1),
                                cos_t, sg_t)
        y_d = _flash_d(dq, dk, dv, 128, 512)

        wb = w_branch[l].astype(BF16)
        wb = wb.at[3].set(wb[3][d_perm, :])
        ys = [y.reshape(rows, BRANCH_W) for y in (y_a, y_b, y_c, y_d)]
        x2d = _merge(x.reshape(rows, D), nw, ys, w[:, o_gate:], b_gate[l].reshape(1, -1), wb,
                     w_out[l].astype(BF16), 512)

        x2d = _ffn(x2d, norm_ffn[l].reshape(1, D), w_ff1[l].astype(BF16), w_ff2[l].astype(BF16),
                   norm_final.reshape(1, D), l == depth - 1, 512)
        x = x2d.reshape(B, S, D)
    return x
```

```python
import functools
import math

import numpy as np
import jax
import jax.numpy as jnp
from jax import lax
from jax.experimental import pallas as pl
from jax.experimental.pallas import tpu as pltpu

F32 = jnp.float32
BF16 = jnp.bfloat16

D_MODEL = 1024
GRID_W = 64
HEAD_DIM = 64
PAIR_W = 2 * HEAD_DIM
N_BRANCH = 4
BRANCH_W = 512
A_HEADS = 4
B_PATTERNS = ((128, 1), (512, 4), (2048, 16))
B_HEADS = 8
B_RADIUS = 64
C_HEADS = 8
NA_KH = 8
NA_KW = 16
D_HEADS = 8
D_KV_HEADS = 2
ROPE_THETA = 10000.0
D_FF = 4 * D_MODEL
EPS = 1e-6
NEG_INF = -1e30
LOG2E = math.log2(math.e)
QK_SCALE = HEAD_DIM ** -0.5

VMEM_LIMIT_BYTES = 56 * 1024 * 1024


def _cparams(*sem):
    return pltpu.CompilerParams(dimension_semantics=sem, vmem_limit_bytes=VMEM_LIMIT_BYTES)


def _const_spec(shape):
    nd = len(shape)
    return pl.BlockSpec(shape, lambda *_: (0,) * nd)


def _rms(x, g):
    return x * lax.rsqrt(jnp.mean(x * x, axis=-1, keepdims=True) + EPS) * g


LANES = 128


def _normed_rows(x_ref, nw_ref, hs_ref, xs_ref, dil):
    rows, width = x_ref.shape
    sub = rows // dil
    h = _rms(x_ref[...], nw_ref[...])
    if dil == 1:
        hs_ref[...] = h.astype(BF16)
        return
    for c in range(width // LANES):
        xs_ref[c] = h[:, c * LANES:(c + 1) * LANES]
    for j in range(dil):
        for c in range(width // LANES):
            hs_ref[j * sub:(j + 1) * sub, c * LANES:(c + 1) * LANES] = (
                xs_ref[c, pl.ds(j, sub, stride=dil), :].astype(BF16))


ONES_ROWS = 16
VT_ROWS = PAIR_W + ONES_ROWS


def _proj_kernel(x_ref, nw_ref, w_ref, cs_ref, q_ref, k_ref, v_ref, hs_ref, xs_ref, *, dil, v_ones):
    _normed_rows(x_ref, nw_ref, hs_ref, xs_ref, dil)
    sub = x_ref.shape[0] // dil
    h = hs_ref[...]
    for idx, (o_ref, transposed) in enumerate(((q_ref, True), (k_ref, False), (v_ref, True))):
        lo = idx * BRANCH_W
        acc = jnp.dot(h, w_ref[:, lo:lo + BRANCH_W], preferred_element_type=F32)
        acc = acc * cs_ref[:, lo:lo + BRANCH_W]
        for j in range(dil):
            part = acc[j * sub:(j + 1) * sub, :]
            if o_ref is v_ref and v_ones:
                pt = part.T.astype(BF16)
                for p in range(BRANCH_W // PAIR_W):
                    o_ref[j, p * VT_ROWS:p * VT_ROWS + PAIR_W, :] = pt[p * PAIR_W:(p + 1) * PAIR_W, :]
                    o_ref[j, p * VT_ROWS + PAIR_W:(p + 1) * VT_ROWS, :] = jnp.ones((ONES_ROWS, sub), BF16)
            else:
                o_ref[j] = (part.T if transposed else part).astype(o_ref.dtype)


def _project(x, nw, w, colscale, dil, v_ones=False):
    B, S, D = x.shape
    L = S // dil
    R = max(1024, 128 * dil)
    R = min(R, S)
    sub = R // dil
    N = w.shape[1]
    vrows = (BRANCH_W // PAIR_W) * VT_ROWS if v_ones else BRANCH_W
    t_shape = jax.ShapeDtypeStruct((B, dil, BRANCH_W, L), BF16)
    n_shape = jax.ShapeDtypeStruct((B, dil, L, BRANCH_W), BF16)
    v_shape = jax.ShapeDtypeStruct((B, dil, vrows, L), BF16)
    t_spec = pl.BlockSpec((None, dil, BRANCH_W, sub), lambda b, i: (b, 0, 0, i))
    n_spec = pl.BlockSpec((None, dil, sub, BRANCH_W), lambda b, i: (b, 0, i, 0))
    v_spec = pl.BlockSpec((None, dil, vrows, sub), lambda b, i: (b, 0, 0, i))
    return pl.pallas_call(
        functools.partial(_proj_kernel, dil=dil, v_ones=v_ones),
        grid=(B, S // R),
        in_specs=[pl.BlockSpec((None, R, D), lambda b, i: (b, i, 0)),
                  _const_spec((1, D)), _const_spec((D, N)), _const_spec((1, N))],
        out_specs=[t_spec, n_spec, v_spec],
        out_shape=[t_shape, n_shape, v_shape],
        scratch_shapes=[pltpu.VMEM((R, D), BF16),
                        pltpu.VMEM((D // LANES, R if dil > 1 else 8, LANES), F32)],
        compiler_params=_cparams("parallel", "parallel"),
        name=f"proj_dil{dil}",
    )(x, nw, w, colscale)


def _norm_rope_t(yt, g, cos, sg):
    outs = []
    for h in range(yt.shape[0] // HEAD_DIM):
        yh = yt[h * HEAD_DIM:(h + 1) * HEAD_DIM, :]
        yn = yh * lax.rsqrt(jnp.mean(yh * yh, axis=0, keepdims=True) + EPS) * g
        swapped = jnp.concatenate([yn[16:32], yn[0:16], yn[48:64], yn[32:48]], axis=0)
        outs.append(yn * cos + swapped * sg)
    return jnp.concatenate(outs, axis=0)


def _proj_d_kernel(x_ref, nw_ref, w_ref, gq_ref, gk_ref, cos_ref, sg_ref, q_ref, k_ref, v_ref, hs_ref):
    _normed_rows(x_ref, nw_ref, hs_ref, None, 1)
    h = hs_ref[...]
    cos = cos_ref[...]
    sg = sg_ref[...]
    qw = D_HEADS * HEAD_DIM
    kw = D_KV_HEADS * HEAD_DIM
    qt = jnp.dot(h, w_ref[:, 0:qw], preferred_element_type=F32).T
    q_ref[...] = (_norm_rope_t(qt, gq_ref[...], cos, sg) * (QK_SCALE * LOG2E)).astype(BF16)
    kt = jnp.dot(h, w_ref[:, qw:qw + kw], preferred_element_type=F32).T
    k_ref[...] = _norm_rope_t(kt, gk_ref[...], cos, sg).T.astype(BF16)
    v_ref[0:kw, :] = jnp.dot(h, w_ref[:, qw + kw:qw + 2 * kw], preferred_element_type=F32).T.astype(BF16)
    v_ref[kw:, :] = jnp.ones((ONES_ROWS, v_ref.shape[1]), BF16)


def _project_d(x, nw, w, gq, gk, cos_t, sg_t):
    B, S, D = x.shape
    R = min(1024, S)
    qw = D_HEADS * HEAD_DIM
    kw = D_KV_HEADS * HEAD_DIM
    return pl.pallas_call(
        _proj_d_kernel,
        grid=(B, S // R),
        in_specs=[pl.BlockSpec((None, R, D), lambda b, i: (b, i, 0)),
                  _const_spec((1, D)), _const_spec((D, qw + 2 * kw)),
                  _const_spec((HEAD_DIM, 1)), _const_spec((HEAD_DIM, 1)),
                  pl.BlockSpec((HEAD_DIM, R), lambda b, i: (0, i)),
                  pl.BlockSpec((HEAD_DIM, R), lambda b, i: (0, i))],
        out_specs=[pl.BlockSpec((None, qw, R), lambda b, i: (b, 0, i)),
                   pl.BlockSpec((None, R, kw), lambda b, i: (b, i, 0)),
                   pl.BlockSpec((None, VT_ROWS, R), lambda b, i: (b, 0, i))],
        out_shape=[jax.ShapeDtypeStruct((B, qw, S), BF16),
                   jax.ShapeDtypeStruct((B, S, kw), BF16),
                   jax.ShapeDtypeStruct((B, VT_ROWS, S), BF16)],
        scratch_shapes=[pltpu.VMEM((R, D), BF16)],
        compiler_params=_cparams("parallel", "parallel"),
        name="proj_gqa",
    )(x, nw, w, gq, gk, cos_t, sg_t)


FLASH_COLS = 256


def _group_spans(c0, width):
    spans, c = [], c0
    while c < c0 + width:
        g, lo = divmod(c, FLASH_COLS)
        n = min(FLASH_COLS - lo, c0 + width - c)
        spans.append((g, lo, n))
        c += n
    return spans


def _read_cols(ref, rows, c0, width):
    parts = [ref[g, rows, lo:lo + n] for g, lo, n in _group_spans(c0, width)]
    return parts[0] if len(parts) == 1 else jnp.concatenate(parts, axis=1)


def _stack_pair_queries(q2_ref, qt_ref, nchunk, tq):
    row = lax.broadcasted_iota(jnp.int32, (PAIR_W, tq), 0)
    for c in range(nchunk):
        qc = qt_ref[c * PAIR_W:(c + 1) * PAIR_W, :].astype(F32)
        for half, keep in enumerate((row < HEAD_DIM, row >= HEAD_DIM)):
            val = jnp.where(keep, qc, 0.0).astype(BF16)
            off = 0
            for g, lo, n in _group_spans((2 * c + half) * tq, tq):
                q2_ref[g, :, lo:lo + n] = val[:, off:off + n]
                off += n


def _flash_scratch(M, tk):
    G = M // FLASH_COLS
    stat = pltpu.VMEM((G, 1, FLASH_COLS), F32)
    probs = pltpu.VMEM((G, tk, FLASH_COLS), BF16)
    return [pltpu.VMEM((G, PAIR_W, FLASH_COLS), BF16), stat, stat,
            pltpu.VMEM((G, VT_ROWS, FLASH_COLS), F32), probs,
            pltpu.VMEM((G, tk, FLASH_COLS), F32), pltpu.VMEM((G, tk, FLASH_COLS), F32), stat, stat,
            probs, stat, stat]


def _flash_loop(k_ref, vt_ref, q2_ref, m_ref, a_ref, acc_ref, p_ref, sa_ref, sb_ref, ma_ref, mb_ref,
                *, tq, tk, seq, bias_fn):
    ngroups = q2_ref.shape[0]
    nk = seq // tk
    m_ref[...] = jnp.full(m_ref.shape, NEG_INF, F32)
    acc_ref[...] = jnp.zeros(acc_ref.shape, F32)

    def scores(j, g, s_ref, mx_ref):
        k = k_ref[pl.ds(pl.multiple_of(j * tk, tk), tk), :]
        s = jnp.dot(k, q2_ref[g], preferred_element_type=F32)
        if bias_fn is not None:
            tile, shift, _ = bias_fn(j, (g * FLASH_COLS) % tq)
            s = s + tile + shift
        s_ref[g] = s
        mx_ref[g] = jnp.max(s, axis=0, keepdims=True)

    def probs(g, s_ref, mx_ref):
        m_old = m_ref[g]
        m_new = jnp.maximum(m_old, mx_ref[g])
        m_ref[g] = m_new
        a_ref[g] = jnp.exp2(m_old - m_new)
        p_ref[g] = jnp.exp2((s_ref[g] - m_new).astype(BF16))

    def values(j, g):
        vt = vt_ref[:, pl.ds(pl.multiple_of(j * tk, tk), tk)]
        acc_ref[g] = acc_ref[g] * a_ref[g] + jnp.dot(vt, p_ref[g], preferred_element_type=F32)

    def step(j, cur, nxt):
        for g in range(ngroups):
            if nxt is not None:
                scores(j + 1, g, *nxt)
            probs(g, *cur)
            if g > 0:
                values(j, g - 1)
        values(j, ngroups - 1)

    buf_a, buf_b = (sa_ref, ma_ref), (sb_ref, mb_ref)
    for g in range(ngroups):
        scores(0, g, *buf_a)

    def body(i, carry):
        step(2 * i, buf_a, buf_b)
        step(2 * i + 1, buf_b, buf_a)
        return carry

    lax.fori_loop(0, nk // 2 - 1, body, 0)
    step(nk - 2, buf_a, buf_b)
    step(nk - 1, buf_b, None)


LAG_LIMIT = 64.0


def _flash_attend(k_ref, vt_ref, q2_ref, m_ref, r_ref, acc_ref, pa_ref, sa_ref, sb_ref, ra_ref, rb_ref,
                  pb_ref, lag_ref, mu_ref, *, tq, tk, seq, bias_fn):
    ngroups = q2_ref.shape[0]
    nk = seq // tk

    def block_scores(j, g):
        k = k_ref[pl.ds(pl.multiple_of(j * tk, tk), tk), :]
        s = jnp.dot(k, q2_ref[g], preferred_element_type=F32)
        if bias_fn is None:
            return s, 0.0, 0.0
        tile, shift, bias_max = bias_fn(j, (g * FLASH_COLS) % tq)
        return s + tile, shift, bias_max

    for g in range(ngroups):
        s, shift, bias_max = block_scores(0, g)
        first = jnp.max(s, axis=0, keepdims=True) + shift
        m_ref[g] = first
        mu_ref[g] = first - bias_max
        r_ref[g] = first
    acc_ref[...] = jnp.zeros(acc_ref.shape, F32)
    lag_ref[...] = jnp.zeros(lag_ref.shape, F32)

    def scores(j, g, p_ref, rp_ref):
        s, shift, bias_max = block_scores(j, g)
        m_old = m_ref[g]
        ref = jnp.maximum(m_old, mu_ref[g] + bias_max)
        rp_ref[g] = ref
        p_ref[g] = jnp.exp2((s - (ref - shift)).astype(BF16))
        mx = jnp.max(s, axis=0, keepdims=True) + shift
        m_new = jnp.maximum(m_old, mx)
        lag_ref[g] = jnp.maximum(lag_ref[g], jnp.maximum(mx - ref, ref - m_new))
        m_ref[g] = m_new
        mu_ref[g] = jnp.maximum(mu_ref[g], mx - bias_max)

    def values(j, g, p_ref, rp_ref):
        vt = vt_ref[:, pl.ds(pl.multiple_of(j * tk, tk), tk)]
        ref = rp_ref[g]
        acc_ref[g] = (acc_ref[g] * jnp.exp2(r_ref[g] - ref)
                      + jnp.dot(vt, p_ref[g], preferred_element_type=F32))
        r_ref[g] = ref

    def step(j, cur, nxt):
        for g in range(ngroups):
            if nxt is not None:
                scores(j + 1, g, *nxt)
            values(j, g, *cur)

    buf_a, buf_b = (pa_ref, ra_ref), (pb_ref, rb_ref)
    for g in range(ngroups):
        scores(0, g, *buf_a)

    def body(i, carry):
        step(2 * i, buf_a, buf_b)
        step(2 * i + 1, buf_b, buf_a)
        return carry

    lax.fori_loop(0, nk // 2 - 1, body, 0)
    step(nk - 2, buf_a, buf_b)
    step(nk - 1, buf_b, None)

    @pl.when(jnp.max(lag_ref[...]) > LAG_LIMIT)
    def _():
        _flash_loop(k_ref, vt_ref, q2_ref, m_ref, r_ref, acc_ref, pa_ref, sa_ref, sb_ref, ra_ref, rb_ref,
                    tq=tq, tk=tk, seq=seq, bias_fn=bias_fn)


def _flash_a_kernel(slope_ref, lam_ref, g_ref, qt_ref, k_ref, vt_ref, o_ref,
                    q2_ref, m_ref, a_ref, acc_ref, *bufs, tq, tk, seq, lam_init):
    _stack_pair_queries(q2_ref, qt_ref, 1, tq)
    *bufs, tiles_ref = bufs
    negc2 = -LOG2E * slope_ref[pl.program_id(1)]
    q0 = pl.program_id(2) * tq
    d0 = (lax.broadcasted_iota(jnp.int32, (tk, FLASH_COLS), 1)
          - lax.broadcasted_iota(jnp.int32, (tk, FLASH_COLS), 0)).astype(F32)
    tiles_ref[0] = d0 * negc2
    tiles_ref[1] = d0 * (-negc2)
    for n in range(tk // FLASH_COLS):
        tiles_ref[2 + n] = jnp.abs(d0 + float(n * FLASH_COLS)) * negc2
    qcol = lax.broadcasted_iota(jnp.int32, (1, FLASH_COLS), 1).astype(F32)

    def bias_fn(j, qcol0):
        delta = q0 + qcol0 - j * tk
        right, left = delta >= tk, delta <= -FLASH_COLS
        idx = jnp.where(right, 0, jnp.where(left, 1, 2 + jnp.clip(delta // FLASH_COLS, 0, tk // FLASH_COLS - 1)))
        shift = jnp.where(right | left, negc2 * jnp.abs(delta).astype(F32), 0.0)
        t = qcol + delta.astype(F32)
        nearest = jnp.maximum(jnp.maximum(t - float(tk - 1), -t), 0.0)
        return tiles_ref[idx], shift, negc2 * nearest

    _flash_attend(k_ref, vt_ref, q2_ref, m_ref, a_ref, acc_ref, *bufs,
                tq=tq, tk=tk, seq=seq, bias_fn=bias_fn)

    lp = lam_ref[...]
    lam = (jnp.exp(jnp.sum(lp[0:1] * lp[1:2], axis=1, keepdims=True))
           - jnp.exp(jnp.sum(lp[2:3] * lp[3:4], axis=1, keepdims=True)) + lam_init)
    vals, sums = slice(0, PAIR_W), slice(PAIR_W, PAIR_W + 1)
    o = (_read_cols(acc_ref, vals, 0, tq) / _read_cols(acc_ref, sums, 0, tq)
         - lam * (_read_cols(acc_ref, vals, tq, tq) / _read_cols(acc_ref, sums, tq, tq)))
    o = o * lax.rsqrt(jnp.mean(o * o, axis=0, keepdims=True) + EPS) * g_ref[...]
    o_ref[...] = (o * (1.0 - lam_init)).T.astype(o_ref.dtype)


def _flash_a(slopes, lamp, subg, qt, k, vt, lam_init, tq, tk):
    B, _, S = qt.shape
    tq, tk = min(tq, S), min(tk, S)
    M = 2 * tq
    return pl.pallas_call(
        functools.partial(_flash_a_kernel, tq=tq, tk=tk, seq=S, lam_init=lam_init),
        grid=(B, A_HEADS, S // tq),
        in_specs=[pl.BlockSpec(memory_space=pltpu.SMEM),
                  _const_spec((4, HEAD_DIM)), _const_spec((PAIR_W, 1)),
                  pl.BlockSpec((None, PAIR_W, tq), lambda b, h, i: (b, h, i)),
                  pl.BlockSpec((None, S, PAIR_W), lambda b, h, i: (b, 0, h)),
                  pl.BlockSpec((None, VT_ROWS, S), lambda b, h, i: (b, h, 0))],
        out_specs=pl.BlockSpec((None, tq, PAIR_W), lambda b, h, i: (b, i, h)),
        out_shape=jax.ShapeDtypeStruct((B, S, A_HEADS * PAIR_W), BF16),
        scratch_shapes=_flash_scratch(M, tk) + [pltpu.VMEM((2 + tk // FLASH_COLS, tk, FLASH_COLS), F32)],
        compiler_params=_cparams("parallel", "parallel", "parallel"),
        name="flash_diff",
    )(slopes, lamp, subg, qt, k, vt)


def _flash_d_kernel(qt_ref, k_ref, vt_ref, o_ref, q2_ref, m_ref, a_ref, acc_ref, *bufs, tq, tk, seq):
    nchunk = qt_ref.shape[0] // PAIR_W
    _stack_pair_queries(q2_ref, qt_ref, nchunk, tq)
    _flash_attend(k_ref, vt_ref, q2_ref, m_ref, a_ref, acc_ref, *bufs,
                tq=tq, tk=tk, seq=seq, bias_fn=None)
    sums = slice(PAIR_W, PAIR_W + 1)
    for c in range(nchunk):
        top, bot = (2 * c) * tq, (2 * c + 1) * tq
        ot = jnp.concatenate(
            [_read_cols(acc_ref, slice(0, HEAD_DIM), top, tq) / _read_cols(acc_ref, sums, top, tq),
             _read_cols(acc_ref, slice(HEAD_DIM, PAIR_W), bot, tq) / _read_cols(acc_ref, sums, bot, tq)],
            axis=0)
        o_ref[:, c * PAIR_W:(c + 1) * PAIR_W] = ot.T.astype(o_ref.dtype)


def _flash_d(qt, k, vt, tq, tk):
    B, qw, S = qt.shape
    tq, tk = min(tq, S), min(tk, S)
    M = 2 * (qw // PAIR_W) * tq
    return pl.pallas_call(
        functools.partial(_flash_d_kernel, tq=tq, tk=tk, seq=S),
        grid=(B, S // tq),
        in_specs=[pl.BlockSpec((None, qw, tq), lambda b, i: (b, 0, i)),
                  pl.BlockSpec((None, S, PAIR_W), lambda b, i: (b, 0, 0)),
                  pl.BlockSpec((None, VT_ROWS, S), lambda b, i: (b, 0, 0))],
        out_specs=pl.BlockSpec((None, tq, qw), lambda b, i: (b, i, 0)),
        out_shape=jax.ShapeDtypeStruct((B, S, qw), BF16),
        scratch_shapes=_flash_scratch(M, tk),
        compiler_params=_cparams("parallel", "parallel"),
        name="flash_gqa",
    )(qt, k, vt)


WIN_TQ = 128
WIN_UNROLL = 8


def _win_kernel(tab_ref, qt_ref, k_ref, vt_ref, bias_ref, o_ref, *lse_refs, width, nblocks):
    row = lax.broadcasted_iota(jnp.int32, (PAIR_W, WIN_TQ), 0)
    unroll = min(WIN_UNROLL, nblocks)

    def scores(i):
        ws = pl.multiple_of(tab_ref[i, 0], 128)
        q0 = pl.multiple_of(i * WIN_TQ, WIN_TQ)
        qc = qt_ref[:, pl.ds(q0, WIN_TQ)].astype(F32)
        q2 = jnp.concatenate([jnp.where(row < HEAD_DIM, qc, 0.0),
                              jnp.where(row >= HEAD_DIM, qc, 0.0)], axis=1).astype(BF16)
        s = jnp.dot(k_ref[pl.ds(ws, width), :], q2, preferred_element_type=F32)
        return ws, q0, s + bias_ref[tab_ref[i, 1]]

    def softmax(s):
        m = jnp.max(s, axis=0, keepdims=True)
        p = jnp.exp2(s - m)
        return m, jnp.sum(p, axis=0, keepdims=True), p.astype(BF16)

    def finish(q0, m, l, acc):
        rl = 1.0 / l
        ot = jnp.concatenate([acc[0:HEAD_DIM, 0:WIN_TQ] * rl[:, 0:WIN_TQ],
                              acc[HEAD_DIM:PAIR_W, WIN_TQ:] * rl[:, WIN_TQ:]], axis=0)
        o_ref[pl.ds(q0, WIN_TQ), :] = ot.T.astype(o_ref.dtype)
        if lse_refs:
            lse = m + jnp.log2(l)
            lt = jnp.concatenate([jnp.broadcast_to(lse[:, 0:WIN_TQ], (HEAD_DIM, WIN_TQ)),
                                  jnp.broadcast_to(lse[:, WIN_TQ:], (HEAD_DIM, WIN_TQ))], axis=0)
            lse_refs[0][pl.ds(q0, WIN_TQ), :] = lt.T

    def body(ib, carry):
        blocks = [scores(ib * unroll + u) for u in range(unroll)]
        stats = [softmax(s) for _, _, s in blocks]
        accs = [jnp.dot(vt_ref[:, pl.ds(ws, width)], p, preferred_element_type=F32)
                for (ws, _, _), (_, _, p) in zip(blocks, stats)]
        for (_, q0, _), (m, l, _), acc in zip(blocks, stats, accs):
            finish(q0, m, l, acc)
        return carry

    lax.fori_loop(0, nblocks // unroll, body, 0)


def _windowed(tab, qt, k, vt, bias, out_dtype, want_lse):
    N, _, L = qt.shape
    nvar, npair, width, _ = bias.shape
    nblocks = L // WIN_TQ
    o_spec = pl.BlockSpec((None, L, PAIR_W), lambda n, p: (n, 0, p))
    out_shape = [jax.ShapeDtypeStruct((N, L, npair * PAIR_W), out_dtype)]
    out_specs = [o_spec]
    if want_lse:
        out_shape.append(jax.ShapeDtypeStruct((N, L, npair * PAIR_W), F32))
        out_specs.append(o_spec)
    return pl.pallas_call(
        functools.partial(_win_kernel, width=width, nblocks=nblocks),
        grid=(N, npair),
        in_specs=[pl.BlockSpec(memory_space=pltpu.SMEM),
                  pl.BlockSpec((None, PAIR_W, L), lambda n, p: (n, p, 0)),
                  pl.BlockSpec((None, L, PAIR_W), lambda n, p: (n, 0, p)),
                  pl.BlockSpec((None, PAIR_W, L), lambda n, p: (n, p, 0)),
                  pl.BlockSpec((nvar, None, width, 2 * WIN_TQ), lambda n, p: (0, p, 0, 0))],
        out_specs=out_specs,
        out_shape=out_shape,
        compiler_params=_cparams("parallel", "parallel"),
        name=f"windowed_w{width}",
    )(tab, qt, k, vt, bias)


def _bmerge_kernel(o0_ref, l0_ref, o1_ref, l1_ref, o2_ref, l2_ref, y_ref, so1, sl1, so2, sl2, *, dils):
    T, width = y_ref.shape
    nchunk = width // LANES
    for (o_ref, l_ref, so, sl, dil) in ((o1_ref, l1_ref, so1, sl1, dils[1]), (o2_ref, l2_ref, so2, sl2, dils[2])):
        sub = T // dil
        for j in range(dil):
            for c in range(nchunk):
                so[c, pl.ds(j, sub, stride=dil), :] = o_ref[j, :, c * LANES:(c + 1) * LANES]
                sl[c, pl.ds(j, sub, stride=dil), :] = l_ref[j, :, c * LANES:(c + 1) * LANES]
    for c in range(nchunk):
        cols = slice(c * LANES, (c + 1) * LANES)
        o0, e0 = o0_ref[0, :, cols], l0_ref[0, :, cols]
        e1, e2 = sl1[c], sl2[c]
        mx = jnp.maximum(jnp.maximum(e0, e1), e2)
        w0, w1, w2 = jnp.exp2(e0 - mx), jnp.exp2(e1 - mx), jnp.exp2(e2 - mx)
        num = w0 * o0 + w1 * so1[c] + w2 * so2[c]
        y_ref[:, cols] = (num / (w0 + w1 + w2)).astype(y_ref.dtype)


def _bmerge(outs, S):
    B = outs[0][0].shape[0]
    dils = tuple(d for _, d in B_PATTERNS)
    T = min(512, S)
    args, in_specs = [], []
    for (o, e), dil in zip(outs, dils):
        spec = pl.BlockSpec((None, dil, T // dil, BRANCH_W), lambda b, t: (b, 0, t, 0))
        args += [o, e]
        in_specs += [spec, spec]
    return pl.pallas_call(
        functools.partial(_bmerge_kernel, dils=dils),
        grid=(B, S // T),
        in_specs=in_specs,
        out_specs=pl.BlockSpec((None, T, BRANCH_W), lambda b, t: (b, t, 0)),
        out_shape=jax.ShapeDtypeStruct((B, S, BRANCH_W), BF16),
        scratch_shapes=[pltpu.VMEM((BRANCH_W // LANES, T, LANES), F32)] * 4,
        compiler_params=_cparams("parallel", "parallel"),
        name="dilated_merge",
    )(*args)


def _merge_kernel(x_ref, nw_ref, ya_ref, yb_ref, yc_ref, yd_ref, wg_ref, bg_ref, wb_ref, wo_ref, o_ref):
    x = x_ref[...]
    h = _rms(x, nw_ref[...]).astype(BF16)
    merged = None
    for n, y_ref in enumerate((ya_ref, yb_ref, yc_ref, yd_ref)):
        cols = slice(n * D_MODEL, (n + 1) * D_MODEL)
        z = jnp.dot(h, wg_ref[:, cols], preferred_element_type=F32) + bg_ref[:, cols]
        gate = 1.0 / (1.0 + jnp.exp(-z))
        t = gate * jnp.dot(y_ref[...], wb_ref[n], preferred_element_type=F32)
        merged = t if merged is None else merged + t
    o_ref[...] = x + jnp.dot(merged.astype(BF16), wo_ref[...], preferred_element_type=F32)


def _merge(x2d, nw, ys, wg, bg, wb, wo, tm):
    Mrows, D = x2d.shape
    tm = min(tm, Mrows)
    row = lambda w: pl.BlockSpec((tm, w), lambda i: (i, 0))
    return pl.pallas_call(
        _merge_kernel,
        grid=(Mrows // tm,),
        in_specs=[row(D), _const_spec((1, D)), row(BRANCH_W), row(BRANCH_W), row(BRANCH_W), row(BRANCH_W),
                  _const_spec(wg.shape), _const_spec(bg.shape), _const_spec(wb.shape), _const_spec(wo.shape)],
        out_specs=row(D),
        out_shape=jax.ShapeDtypeStruct((Mrows, D), F32),
        compiler_params=_cparams("parallel"),
        name="gated_merge",
    )(x2d, nw, *ys, wg, bg, wb, wo)


FFN_CHUNK = 1024


def _ffn_kernel(x_ref, nw_ref, w1_ref, w2_ref, nf_ref, o_ref, *, final):
    x = x_ref[...]
    h = _rms(x, nw_ref[...]).astype(BF16)
    acc = x
    for c0 in range(0, D_FF, FFN_CHUNK):
        u = jnp.maximum(jnp.dot(h, w1_ref[:, c0:c0 + FFN_CHUNK], preferred_element_type=F32), 0.0)
        acc = acc + jnp.dot((u * u).astype(BF16), w2_ref[c0:c0 + FFN_CHUNK, :], preferred_element_type=F32)
    if final:
        acc = _rms(acc, nf_ref[...])
    o_ref[...] = acc


def _ffn(x2d, nw, w1, w2, nf, final, tm):
    Mrows, D = x2d.shape
    tm = min(tm, Mrows)
    row = pl.BlockSpec((tm, D), lambda i: (i, 0))
    return pl.pallas_call(
        functools.partial(_ffn_kernel, final=final),
        grid=(Mrows // tm,),
        in_specs=[row, _const_spec((1, D)), _const_spec(w1.shape), _const_spec(w2.shape), _const_spec((1, D))],
        out_specs=row,
        out_shape=jax.ShapeDtypeStruct((Mrows, D), F32),
        compiler_params=_cparams("parallel"),
        name="relu2_mlp",
    )(x2d, nw, w1, w2, nf)


def _alibi_slopes(n):
    return np.array([2.0 ** (-8.0 * (i + 1) / n) for i in range(n)], dtype=np.float32)


def _pair_tiles(t):
    nvar, nh, w, tq = t.shape
    t = t.reshape(nvar, nh // 2, 2, w, tq)
    return jnp.concatenate([t[:, :, 0], t[:, :, 1]], axis=-1)


def _dilated_tables(L, dil):
    width = 3 * WIN_TQ
    nb = L // WIN_TQ
    ws = np.clip((np.arange(nb) - 1) * WIN_TQ, 0, L - width)
    var = np.where(np.arange(nb) == 0, 0, np.where(np.arange(nb) == nb - 1, 2, 1))
    tab = np.stack([ws, var], axis=1).astype(np.int32)
    c = np.arange(width)[:, None]
    r = np.arange(WIN_TQ)[None, :]
    slopes = _alibi_slopes(B_HEADS)
    tiles = []
    for shift in (0, WIN_TQ, 2 * WIN_TQ):
        rel = c - shift - r
        valid = np.abs(rel) <= B_RADIUS
        bias = -slopes[:, None, None] * (np.abs(rel) * dil).astype(np.float32)[None] * np.float32(LOG2E)
        tiles.append(np.where(valid[None], bias, np.float32(NEG_INF)))
    return jnp.asarray(tab), _pair_tiles(jnp.asarray(np.stack(tiles).astype(np.float32)))


def _na_tables(S, rpb):
    rows = S // GRID_W
    kh = min(NA_KH, rows)
    wrows = 10
    nb = rows // 2
    r0s = 2 * np.arange(nb)
    wsr = np.clip(r0s - kh // 2, 0, rows - wrows)
    tab = np.stack([wsr * GRID_W, (r0s - wsr) // 2], axis=1).astype(np.int32)
    reps = [0, 2, 4, rows - 4, rows - 2]
    krl = np.arange(wrows)
    kc = np.arange(GRID_W)
    qrl = np.arange(2)
    qc = np.arange(GRID_W)
    qstart = np.clip(qc - NA_KW // 2, 0, GRID_W - NA_KW)
    col_ok = (kc[:, None] >= qstart[None, :]) & (kc[:, None] < qstart[None, :] + NA_KW)
    ridx, valid = [], []
    for r0 in reps:
        w0 = int(np.clip(r0 - kh // 2, 0, rows - wrows))
        r = (r0 + qrl)[None, :]
        rs = np.clip(r - kh // 2, 0, rows - kh)
        kr = (w0 + krl)[:, None]
        row_ok = (kr >= rs) & (kr < rs + kh)
        ridx.append(np.clip(kr - r + NA_KH - 1, 0, 2 * NA_KH - 2))
        valid.append(row_ok[:, None, :, None] & col_ok[None, :, None, :])
    ridx, valid = np.stack(ridx), np.stack(valid)
    nh = rpb.shape[0]
    span = GRID_W - NA_KW
    padded = jnp.pad(rpb.astype(F32), ((0, 0), (0, 0), (span, span)), mode="edge")
    by_col = jnp.stack([padded[:, :, GRID_W - 1 - c:2 * GRID_W - 1 - c] for c in range(GRID_W)], axis=-1)
    slabs = jnp.take(by_col, jnp.asarray(ridx.reshape(-1)), axis=1)
    slabs = slabs.reshape(nh, len(reps), wrows, 2, GRID_W, GRID_W).transpose(1, 0, 2, 4, 3, 5)
    tiles = jnp.where(valid[:, None], slabs * LOG2E, NEG_INF)
    tiles = tiles.reshape(len(reps), nh, wrows * GRID_W, 2 * GRID_W)
    return jnp.asarray(tab), _pair_tiles(tiles)


def _rope_tables(S):
    n = HEAD_DIM // 4
    pos = np.arange(S)
    inv = ROPE_THETA ** (-np.arange(n, dtype=np.float32) / n)
    d = np.arange(HEAD_DIM)
    p = np.where((d < HEAD_DIM // 2)[:, None], (pos // GRID_W)[None, :], (pos % GRID_W)[None, :]).astype(np.float32)
    ang = p * inv[d % n][:, None]
    sign = np.where((d % (2 * n)) < n, -1.0, 1.0).astype(np.float32)[:, None]
    return jnp.asarray(np.cos(ang).astype(np.float32)), jnp.asarray((np.sin(ang) * sign).astype(np.float32))


def kernel(x, norm_mix, w_in, b_gate, diff_lambda, diff_subln, na_rpb, qk_norm, w_branch, w_out, norm_ffn,
           w_ff1, w_ff2, norm_final):
    B, S, D = x.shape
    depth = w_in.shape[0]
    rows = B * S

    ones_col = jnp.ones((1, 3 * BRANCH_W), F32)
    qscale_col = ones_col.at[:, 0:BRANCH_W].set(QK_SCALE * LOG2E)
    a_slopes = jnp.asarray(_alibi_slopes(A_HEADS))
    cos_t, sg_t = _rope_tables(S)
    b_tabs = [_dilated_tables(S // dil, dil) for _, dil in B_PATTERNS]
    d_perm = np.array([(g * 4 + c) * HEAD_DIM + d for c in range(4) for g in range(D_KV_HEADS)
                       for d in range(HEAD_DIM)])

    o_a, o_b, o_c, o_d, o_gate = 0, 1536, 6144, 7680, 8448
    for l in range(depth):
        w = w_in[l].astype(BF16)
        nw = norm_mix[l].reshape(1, D)
        lam_init = 0.8 - 0.6 * math.exp(-0.3 * l)

        aq, ak, av = _project(x, nw, w[:, o_a:o_a + 1536], qscale_col, 1, v_ones=True)
        y_a = _flash_a(a_slopes, diff_lambda[l], diff_subln[l].reshape(PAIR_W, 1),
                       aq[:, 0], ak[:, 0], av[:, 0], lam_init, 512, 512)

        b_outs = []
        for g, (_, dil) in enumerate(B_PATTERNS):
            lo = o_b + g * 1536
            bq, bk, bv = _project(x, nw, w[:, lo:lo + 1536], qscale_col, dil)
            L = S // dil
            tab, bias = b_tabs[g]
            o, e = _windowed(tab, bq.reshape(B * dil, BRANCH_W, L), bk.reshape(B * dil, L, BRANCH_W),
                             bv.reshape(B * dil, BRANCH_W, L), bias, F32, True)
            b_outs.append((o.reshape(B, dil, L, BRANCH_W), e.reshape(B, dil, L, BRANCH_W)))
        y_b = _bmerge(b_outs, S)

        cq, ck, cv = _project(x, nw, w[:, o_c:o_c + 1536], qscale_col, 1)
        c_tab, c_bias = _na_tables(S, na_rpb[l])
        (y_c,) = _windowed(c_tab, cq[:, 0], ck[:, 0], cv[:, 0], c_bias, BF16, False)

        wd = jnp.concatenate([w[:, o_d:o_d + 512][:, d_perm], w[:, o_d + 512:o_gate]], axis=1)
        dq, dk, dv = _project_d(x, nw, wd, qk_norm[l, 0].reshape(HEAD_DIM, 1), qk_norm[l, 1].reshape(HEAD_DIM, 1),
                                cos_t, sg_t)
        y_d = _flash_d(dq, dk, dv, 128, 512)

        wb = w_branch[l].astype(BF16)
        wb = wb.at[3].set(wb[3][d_perm, :])
        ys = [y.reshape(rows, BRANCH_W) for y in (y_a, y_b, y_c, y_d)]
        x2d = _merge(x.reshape(rows, D), nw, ys, w[:, o_gate:], b_gate[l].reshape(1, -1), wb,
                     w_out[l].astype(BF16), 512)

        x2d = _ffn(x2d, norm_ffn[l].reshape(1, D), w_ff1[l].astype(BF16), w_ff2[l].astype(BF16),
                   norm_final.reshape(1, D), l == depth - 1, 512)
        x = x2d.reshape(B, S, D)
    return x
```

```python
import functools
import math

import numpy as np
import jax
import jax.numpy as jnp
from jax import lax
from jax.experimental import pallas as pl
from jax.experimental.pallas import tpu as pltpu

F32 = jnp.float32
BF16 = jnp.bfloat16

D_MODEL = 1024
GRID_W = 64
HEAD_DIM = 64
PAIR_W = 2 * HEAD_DIM
N_BRANCH = 4
BRANCH_W = 512
A_HEADS = 4
B_PATTERNS = ((128, 1), (512, 4), (2048, 16))
B_HEADS = 8
B_RADIUS = 64
C_HEADS = 8
NA_KH = 8
NA_KW = 16
D_HEADS = 8
D_KV_HEADS = 2
ROPE_THETA = 10000.0
D_FF = 4 * D_MODEL
EPS = 1e-6
NEG_INF = -1e30
LOG2E = math.log2(math.e)
QK_SCALE = HEAD_DIM ** -0.5

VMEM_LIMIT_BYTES = 56 * 1024 * 1024


def _cparams(*sem):
    return pltpu.CompilerParams(dimension_semantics=sem, vmem_limit_bytes=VMEM_LIMIT_BYTES)


def _const_spec(shape):
    nd = len(shape)
    return pl.BlockSpec(shape, lambda *_: (0,) * nd)


def _rms(x, g):
    return x * lax.rsqrt(jnp.mean(x * x, axis=-1, keepdims=True) + EPS) * g


LANES = 128


def _normed_rows(x_ref, nw_ref, hs_ref, xs_ref, dil):
    rows, width = x_ref.shape
    sub = rows // dil
    h = _rms(x_ref[...], nw_ref[...])
    if dil == 1:
        hs_ref[...] = h.astype(BF16)
        return
    for c in range(width // LANES):
        xs_ref[c] = h[:, c * LANES:(c + 1) * LANES]
    for j in range(dil):
        for c in range(width // LANES):
            hs_ref[j * sub:(j + 1) * sub, c * LANES:(c + 1) * LANES] = (
                xs_ref[c, pl.ds(j, sub, stride=dil), :].astype(BF16))


ONES_ROWS = 16
VT_ROWS = PAIR_W + ONES_ROWS


def _proj_kernel(x_ref, nw_ref, w_ref, cs_ref, q_ref, k_ref, v_ref, hs_ref, xs_ref, *, dil, v_ones):
    _normed_rows(x_ref, nw_ref, hs_ref, xs_ref, dil)
    sub = x_ref.shape[0] // dil
    h = hs_ref[...]
    for idx, (o_ref, transposed) in enumerate(((q_ref, True), (k_ref, False), (v_ref, True))):
        lo = idx * BRANCH_W
        acc = jnp.dot(h, w_ref[:, lo:lo + BRANCH_W], preferred_element_type=F32)
        acc = acc * cs_ref[:, lo:lo + BRANCH_W]
        for j in range(dil):
            part = acc[j * sub:(j + 1) * sub, :]
            if o_ref is v_ref and v_ones:
                pt = part.T.astype(BF16)
                for p in range(BRANCH_W // PAIR_W):
                    o_ref[j, p * VT_ROWS:p * VT_ROWS + PAIR_W, :] = pt[p * PAIR_W:(p + 1) * PAIR_W, :]
                    o_ref[j, p * VT_ROWS + PAIR_W:(p + 1) * VT_ROWS, :] = jnp.ones((ONES_ROWS, sub), BF16)
            else:
                o_ref[j] = (part.T if transposed else part).astype(o_ref.dtype)


def _project(x, nw, w, colscale, dil, v_ones=False):
    B, S, D = x.shape
    L = S // dil
    R = max(1024, 128 * dil)
    R = min(R, S)
    sub = R // dil
    N = w.shape[1]
    vrows = (BRANCH_W // PAIR_W) * VT_ROWS if v_ones else BRANCH_W
    t_shape = jax.ShapeDtypeStruct((B, dil, BRANCH_W, L), BF16)
    n_shape = jax.ShapeDtypeStruct((B, dil, L, BRANCH_W), BF16)
    v_shape = jax.ShapeDtypeStruct((B, dil, vrows, L), BF16)
    t_spec = pl.BlockSpec((None, dil, BRANCH_W, sub), lambda b, i: (b, 0, 0, i))
    n_spec = pl.BlockSpec((None, dil, sub, BRANCH_W), lambda b, i: (b, 0, i, 0))
    v_spec = pl.BlockSpec((None, dil, vrows, sub), lambda b, i: (b, 0, 0, i))
    return pl.pallas_call(
        functools.partial(_proj_kernel, dil=dil, v_ones=v_ones),
        grid=(B, S // R),
        in_specs=[pl.BlockSpec((None, R, D), lambda b, i: (b, i, 0)),
                  _const_spec((1, D)), _const_spec((D, N)), _const_spec((1, N))],
        out_specs=[t_spec, n_spec, v_spec],
        out_shape=[t_shape, n_shape, v_shape],
        scratch_shapes=[pltpu.VMEM((R, D), BF16),
                        pltpu.VMEM((D // LANES, R if dil > 1 else 8, LANES), F32)],
        compiler_params=_cparams("parallel", "parallel"),
        name=f"proj_dil{dil}",
    )(x, nw, w, colscale)


def _norm_rope_t(yt, g, cos, sg):
    outs = []
    for h in range(yt.shape[0] // HEAD_DIM):
        yh = yt[h * HEAD_DIM:(h + 1) * HEAD_DIM, :]
        yn = yh * lax.rsqrt(jnp.mean(yh * yh, axis=0, keepdims=True) + EPS) * g
        swapped = jnp.concatenate([yn[16:32], yn[0:16], yn[48:64], yn[32:48]], axis=0)
        outs.append(yn * cos + swapped * sg)
    return jnp.concatenate(outs, axis=0)


def _proj_d_kernel(x_ref, nw_ref, w_ref, gq_ref, gk_ref, cos_ref, sg_ref, q_ref, k_ref, v_ref, hs_ref):
    _normed_rows(x_ref, nw_ref, hs_ref, None, 1)
    h = hs_ref[...]
    cos = cos_ref[...]
    sg = sg_ref[...]
    qw = D_HEADS * HEAD_DIM
    kw = D_KV_HEADS * HEAD_DIM
    qt = jnp.dot(h, w_ref[:, 0:qw], preferred_element_type=F32).T
    q_ref[...] = (_norm_rope_t(qt, gq_ref[...], cos, sg) * (QK_SCALE * LOG2E)).astype(BF16)
    kt = jnp.dot(h, w_ref[:, qw:qw + kw], preferred_element_type=F32).T
    k_ref[...] = _norm_rope_t(kt, gk_ref[...], cos, sg).T.astype(BF16)
    v_ref[0:kw, :] = jnp.dot(h, w_ref[:, qw + kw:qw + 2 * kw], preferred_element_type=F32).T.astype(BF16)
    v_ref[kw:, :] = jnp.ones((ONES_ROWS, v_ref.shape[1]), BF16)


def _project_d(x, nw, w, gq, gk, cos_t, sg_t):
    B, S, D = x.shape
    R = min(1024, S)
    qw = D_HEADS * HEAD_DIM
    kw = D_KV_HEADS * HEAD_DIM
    return pl.pallas_call(
        _proj_d_kernel,
        grid=(B, S // R),
        in_specs=[pl.BlockSpec((None, R, D), lambda b, i: (b, i, 0)),
                  _const_spec((1, D)), _const_spec((D, qw + 2 * kw)),
                  _const_spec((HEAD_DIM, 1)), _const_spec((HEAD_DIM, 1)),
                  pl.BlockSpec((HEAD_DIM, R), lambda b, i: (0, i)),
                  pl.BlockSpec((HEAD_DIM, R), lambda b, i: (0, i))],
        out_specs=[pl.BlockSpec((None, qw, R), lambda b, i: (b, 0, i)),
                   pl.BlockSpec((None, R, kw), lambda b, i: (b, i, 0)),
                   pl.BlockSpec((None, VT_ROWS, R), lambda b, i: (b, 0, i))],
        out_shape=[jax.ShapeDtypeStruct((B, qw, S), BF16),
                   jax.ShapeDtypeStruct((B, S, kw), BF16),
                   jax.ShapeDtypeStruct((B, VT_ROWS, S), BF16)],
        scratch_shapes=[pltpu.VMEM((R, D), BF16)],
        compiler_params=_cparams("parallel", "parallel"),
        name="proj_gqa",
    )(x, nw, w, gq, gk, cos_t, sg_t)


FLASH_COLS = 256


def _group_spans(c0, width):
    spans, c = [], c0
    while c < c0 + width:
        g, lo = divmod(c, FLASH_COLS)
        n = min(FLASH_COLS - lo, c0 + width - c)
        spans.append((g, lo, n))
        c += n
    return spans


def _read_cols(ref, rows, c0, width):
    parts = [ref[g, rows, lo:lo + n] for g, lo, n in _group_spans(c0, width)]
    return parts[0] if len(parts) == 1 else jnp.concatenate(parts, axis=1)


def _stack_pair_queries(q2_ref, qt_ref, nchunk, tq):
    row = lax.broadcasted_iota(jnp.int32, (PAIR_W, tq), 0)
    for c in range(nchunk):
        qc = qt_ref[c * PAIR_W:(c + 1) * PAIR_W, :].astype(F32)
        for half, keep in enumerate((row < HEAD_DIM, row >= HEAD_DIM)):
            val = jnp.where(keep, qc, 0.0).astype(BF16)
            off = 0
            for g, lo, n in _group_spans((2 * c + half) * tq, tq):
                q2_ref[g, :, lo:lo + n] = val[:, off:off + n]
                off += n


def _flash_scratch(M, tk):
    G = M // FLASH_COLS
    stat = pltpu.VMEM((G, 1, FLASH_COLS), F32)
    probs = pltpu.VMEM((G, tk, FLASH_COLS), BF16)
    return [pltpu.VMEM((G, PAIR_W, FLASH_COLS), BF16), stat, stat,
            pltpu.VMEM((G, VT_ROWS, FLASH_COLS), F32), probs,
            pltpu.VMEM((G, tk, FLASH_COLS), F32), pltpu.VMEM((G, tk, FLASH_COLS), F32), stat, stat,
            probs, stat, stat]


def _flash_loop(k_ref, vt_ref, q2_ref, m_ref, a_ref, acc_ref, p_ref, sa_ref, sb_ref, ma_ref, mb_ref,
                *, tq, tk, seq, bias_fn):
    ngroups = q2_ref.shape[0]
    nk = seq // tk
    m_ref[...] = jnp.full(m_ref.shape, NEG_INF, F32)
    acc_ref[...] = jnp.zeros(acc_ref.shape, F32)

    def scores(j, g, s_ref, mx_ref):
        k = k_ref[pl.ds(pl.multiple_of(j * tk, tk), tk), :]
        s = jnp.dot(k, q2_ref[g], preferred_element_type=F32)
        if bias_fn is not None:
            tile, shift, _ = bias_fn(j, (g * FLASH_COLS) % tq)
            s = s + tile + shift
        s_ref[g] = s
        mx_ref[g] = jnp.max(s, axis=0, keepdims=True)

    def probs(g, s_ref, mx_ref):
        m_old = m_ref[g]
        m_new = jnp.maximum(m_old, mx_ref[g])
        m_ref[g] = m_new
        a_ref[g] = jnp.exp2(m_old - m_new)
        p_ref[g] = jnp.exp2((s_ref[g] - m_new).astype(BF16))

    def values(j, g):
        vt = vt_ref[:, pl.ds(pl.multiple_of(j * tk, tk), tk)]
        acc_ref[g] = acc_ref[g] * a_ref[g] + jnp.dot(vt, p_ref[g], preferred_element_type=F32)

    def step(j, cur, nxt):
        for g in range(ngroups):
            if nxt is not None:
                scores(j + 1, g, *nxt)
            probs(g, *cur)
            if g > 0:
                values(j, g - 1)
        values(j, ngroups - 1)

    buf_a, buf_b = (sa_ref, ma_ref), (sb_ref, mb_ref)
    for g in range(ngroups):
        scores(0, g, *buf_a)

    def body(i, carry):
        step(2 * i, buf_a, buf_b)
        step(2 * i + 1, buf_b, buf_a)
        return carry

    lax.fori_loop(0, nk // 2 - 1, body, 0)
    step(nk - 2, buf_a, buf_b)
    step(nk - 1, buf_b, None)


LAG_LIMIT = 64.0
FLASH_PROBE_ROWS = 128


def _flash_attend(k_ref, vt_ref, q2_ref, m_ref, r_ref, acc_ref, pa_ref, sa_ref, sb_ref, ra_ref, rb_ref,
                  pb_ref, lag_ref, mu_ref, *, tq, tk, seq, bias_fn):
    ngroups = q2_ref.shape[0]
    nk = seq // tk

    def block_scores(j, g, rows=tk):
        k = k_ref[pl.ds(pl.multiple_of(j * tk, tk), rows), :]
        s = jnp.dot(k, q2_ref[g], preferred_element_type=F32)
        if bias_fn is None:
            return s, 0.0, 0.0
        tile, shift, bias_max = bias_fn(j, (g * FLASH_COLS) % tq, rows)
        return s + tile, shift, bias_max

    for g in range(ngroups):
        s, shift, bias_max = block_scores(0, g, FLASH_PROBE_ROWS)
        first = jnp.max(s, axis=0, keepdims=True) + shift
        m_ref[g] = first
        mu_ref[g] = first - bias_max
        r_ref[g] = first
    acc_ref[...] = jnp.zeros(acc_ref.shape, F32)
    lag_ref[...] = jnp.zeros(lag_ref.shape, F32)

    def scores(j, g, p_ref, rp_ref):
        s, shift, bias_max = block_scores(j, g)
        m_old = m_ref[g]
        ref = jnp.maximum(m_old, mu_ref[g] + bias_max)
        rp_ref[g] = ref
        p_ref[g] = jnp.exp2((s - (ref - shift)).astype(BF16))
        mx = jnp.max(s, axis=0, keepdims=True) + shift
        m_new = jnp.maximum(m_old, mx)
        lag_ref[g] = jnp.maximum(lag_ref[g], jnp.maximum(mx - ref, ref - m_new))
        m_ref[g] = m_new
        mu_ref[g] = jnp.maximum(mu_ref[g], mx - bias_max)

    def values(j, g, p_ref, rp_ref):
        vt = vt_ref[:, pl.ds(pl.multiple_of(j * tk, tk), tk)]
        ref = rp_ref[g]
        acc_ref[g] = (acc_ref[g] * jnp.exp2(r_ref[g] - ref)
                      + jnp.dot(vt, p_ref[g], preferred_element_type=F32))
        r_ref[g] = ref

    def step(j, cur, nxt):
        for g in range(ngroups):
            if nxt is not None:
                scores(j + 1, g, *nxt)
            values(j, g, *cur)

    buf_a, buf_b = (pa_ref, ra_ref), (pb_ref, rb_ref)
    for g in range(ngroups):
        scores(0, g, *buf_a)

    def body(i, carry):
        step(2 * i, buf_a, buf_b)
        step(2 * i + 1, buf_b, buf_a)
        return carry

    lax.fori_loop(0, nk // 2 - 1, body, 0)
    step(nk - 2, buf_a, buf_b)
    step(nk - 1, buf_b, None)

    @pl.when(jnp.max(lag_ref[...]) > LAG_LIMIT)
    def _():
        _flash_loop(k_ref, vt_ref, q2_ref, m_ref, r_ref, acc_ref, pa_ref, sa_ref, sb_ref, ra_ref, rb_ref,
                    tq=tq, tk=tk, seq=seq, bias_fn=bias_fn)


def _flash_a_kernel(slope_ref, lam_ref, g_ref, qt_ref, k_ref, vt_ref, o_ref,
                    q2_ref, m_ref, a_ref, acc_ref, *bufs, tq, tk, seq, lam_init):
    _stack_pair_queries(q2_ref, qt_ref, 1, tq)
    *bufs, tiles_ref = bufs
    negc2 = -LOG2E * slope_ref[pl.program_id(1)]
    q0 = pl.program_id(2) * tq
    d0 = (lax.broadcasted_iota(jnp.int32, (tk, FLASH_COLS), 1)
          - lax.broadcasted_iota(jnp.int32, (tk, FLASH_COLS), 0)).astype(F32)
    tiles_ref[0] = d0 * negc2
    tiles_ref[1] = d0 * (-negc2)
    for n in range(tk // FLASH_COLS):
        tiles_ref[2 + n] = jnp.abs(d0 + float(n * FLASH_COLS)) * negc2
    qcol = lax.broadcasted_iota(jnp.int32, (1, FLASH_COLS), 1).astype(F32)

    def bias_fn(j, qcol0, rows=tk):
        delta = q0 + qcol0 - j * tk
        right, left = delta >= tk, delta <= -FLASH_COLS
        idx = jnp.where(right, 0, jnp.where(left, 1, 2 + jnp.clip(delta // FLASH_COLS, 0, tk // FLASH_COLS - 1)))
        shift = jnp.where(right | left, negc2 * jnp.abs(delta).astype(F32), 0.0)
        t = qcol + delta.astype(F32)
        nearest = jnp.maximum(jnp.maximum(t - float(rows - 1), -t), 0.0)
        return tiles_ref[idx, 0:rows, :], shift, negc2 * nearest

    _flash_attend(k_ref, vt_ref, q2_ref, m_ref, a_ref, acc_ref, *bufs,
                tq=tq, tk=tk, seq=seq, bias_fn=bias_fn)

    lp = lam_ref[...]
    lam = (jnp.exp(jnp.sum(lp[0:1] * lp[1:2], axis=1, keepdims=True))
           - jnp.exp(jnp.sum(lp[2:3] * lp[3:4], axis=1, keepdims=True)) + lam_init)
    vals, sums = slice(0, PAIR_W), slice(PAIR_W, PAIR_W + 1)
    o = (_read_cols(acc_ref, vals, 0, tq) / _read_cols(acc_ref, sums, 0, tq)
         - lam * (_read_cols(acc_ref, vals, tq, tq) / _read_cols(acc_ref, sums, tq, tq)))
    o = o * lax.rsqrt(jnp.mean(o * o, axis=0, keepdims=True) + EPS) * g_ref[...]
    o_ref[...] = (o * (1.0 - lam_init)).T.astype(o_ref.dtype)


def _flash_a(slopes, lamp, subg, qt, k, vt, lam_init, tq, tk):
    B, _, S = qt.shape
    tq, tk = min(tq, S), min(tk, S)
    M = 2 * tq
    return pl.pallas_call(
        functools.partial(_flash_a_kernel, tq=tq, tk=tk, seq=S, lam_init=lam_init),
        grid=(B, A_HEADS, S // tq),
        in_specs=[pl.BlockSpec(memory_space=pltpu.SMEM),
                  _const_spec((4, HEAD_DIM)), _const_spec((PAIR_W, 1)),
                  pl.BlockSpec((None, PAIR_W, tq), lambda b, h, i: (b, h, i)),
                  pl.BlockSpec((None, S, PAIR_W), lambda b, h, i: (b, 0, h)),
                  pl.BlockSpec((None, VT_ROWS, S), lambda b, h, i: (b, h, 0))],
        out_specs=pl.BlockSpec((None, tq, PAIR_W), lambda b, h, i: (b, i, h)),
        out_shape=jax.ShapeDtypeStruct((B, S, A_HEADS * PAIR_W), BF16),
        scratch_shapes=_flash_scratch(M, tk) + [pltpu.VMEM((2 + tk // FLASH_COLS, tk, FLASH_COLS), F32)],
        compiler_params=_cparams("parallel", "parallel", "parallel"),
        name="flash_diff",
    )(slopes, lamp, subg, qt, k, vt)


def _flash_d_kernel(qt_ref, k_ref, vt_ref, o_ref, q2_ref, m_ref, a_ref, acc_ref, *bufs, tq, tk, seq):
    nchunk = qt_ref.shape[0] // PAIR_W
    _stack_pair_queries(q2_ref, qt_ref, nchunk, tq)
    _flash_attend(k_ref, vt_ref, q2_ref, m_ref, a_ref, acc_ref, *bufs,
                tq=tq, tk=tk, seq=seq, bias_fn=None)
    sums = slice(PAIR_W, PAIR_W + 1)
    for c in range(nchunk):
        top, bot = (2 * c) * tq, (2 * c + 1) * tq
        ot = jnp.concatenate(
            [_read_cols(acc_ref, slice(0, HEAD_DIM), top, tq) / _read_cols(acc_ref, sums, top, tq),
             _read_cols(acc_ref, slice(HEAD_DIM, PAIR_W), bot, tq) / _read_cols(acc_ref, sums, bot, tq)],
            axis=0)
        o_ref[:, c * PAIR_W:(c + 1) * PAIR_W] = ot.T.astype(o_ref.dtype)


def _flash_d(qt, k, vt, tq, tk):
    B, qw, S = qt.shape
    tq, tk = min(tq, S), min(tk, S)
    M = 2 * (qw // PAIR_W) * tq
    return pl.pallas_call(
        functools.partial(_flash_d_kernel, tq=tq, tk=tk, seq=S),
        grid=(B, S // tq),
        in_specs=[pl.BlockSpec((None, qw, tq), lambda b, i: (b, 0, i)),
                  pl.BlockSpec((None, S, PAIR_W), lambda b, i: (b, 0, 0)),
                  pl.BlockSpec((None, VT_ROWS, S), lambda b, i: (b, 0, 0))],
        out_specs=pl.BlockSpec((None, tq, qw), lambda b, i: (b, i, 0)),
        out_shape=jax.ShapeDtypeStruct((B, S, qw), BF16),
        scratch_shapes=_flash_scratch(M, tk),
        compiler_params=_cparams("parallel", "parallel"),
        name="flash_gqa",
    )(qt, k, vt)


WIN_TQ = 128
WIN_UNROLL = 8


def _win_kernel(tab_ref, qt_ref, k_ref, vt_ref, bias_ref, o_ref, *lse_refs, width, nblocks, unroll):
    row = lax.broadcasted_iota(jnp.int32, (PAIR_W, WIN_TQ), 0)
    pairs = qt_ref.shape[0] // PAIR_W

    def scores(pi, i):
        lanes = slice(pi * PAIR_W, (pi + 1) * PAIR_W)
        ws = pl.multiple_of(tab_ref[i, 0], 128)
        q0 = pl.multiple_of(i * WIN_TQ, WIN_TQ)
        qc = qt_ref[lanes, pl.ds(q0, WIN_TQ)].astype(F32)
        q2 = jnp.concatenate([jnp.where(row < HEAD_DIM, qc, 0.0),
                              jnp.where(row >= HEAD_DIM, qc, 0.0)], axis=1).astype(BF16)
        s = jnp.dot(k_ref[pl.ds(ws, width), lanes], q2, preferred_element_type=F32)
        return lanes, ws, q0, s + bias_ref[tab_ref[i, 1], pi]

    def softmax(s):
        m = jnp.max(s, axis=0, keepdims=True)
        p = jnp.exp2(s - m)
        return m, jnp.sum(p, axis=0, keepdims=True), p.astype(BF16)

    def finish(lanes, q0, m, l, acc):
        rl = 1.0 / l
        ot = jnp.concatenate([acc[0:HEAD_DIM, 0:WIN_TQ] * rl[:, 0:WIN_TQ],
                              acc[HEAD_DIM:PAIR_W, WIN_TQ:] * rl[:, WIN_TQ:]], axis=0)
        o_ref[pl.ds(q0, WIN_TQ), lanes] = ot.T.astype(o_ref.dtype)
        if lse_refs:
            lse = m + jnp.log2(l)
            lt = jnp.concatenate([jnp.broadcast_to(lse[:, 0:WIN_TQ], (HEAD_DIM, WIN_TQ)),
                                  jnp.broadcast_to(lse[:, WIN_TQ:], (HEAD_DIM, WIN_TQ))], axis=0)
            lse_refs[0][pl.ds(q0, WIN_TQ), lanes] = lt.T

    def body(ib, carry):
        blocks = [scores(pi, ib * unroll + u) for pi in range(pairs) for u in range(unroll)]
        stats = [softmax(s) for *_, s in blocks]
        accs = [jnp.dot(vt_ref[lanes, pl.ds(ws, width)], p, preferred_element_type=F32)
                for (lanes, ws, _, _), (_, _, p) in zip(blocks, stats)]
        for (lanes, _, q0, _), (m, l, _), acc in zip(blocks, stats, accs):
            finish(lanes, q0, m, l, acc)
        return carry

    lax.fori_loop(0, nblocks // unroll, body, 0)


def _windowed(tab, qt, k, vt, bias, out_dtype, want_lse):
    N, _, L = qt.shape
    nvar, npair, width, _ = bias.shape
    nblocks = L // WIN_TQ
    unroll = min(WIN_UNROLL, nblocks)
    pp = min(npair, WIN_UNROLL // unroll)
    o_spec = pl.BlockSpec((None, L, pp * PAIR_W), lambda n, p: (n, 0, p))
    out_shape = [jax.ShapeDtypeStruct((N, L, npair * PAIR_W), out_dtype)]
    out_specs = [o_spec]
    if want_lse:
        out_shape.append(jax.ShapeDtypeStruct((N, L, npair * PAIR_W), F32))
        out_specs.append(o_spec)
    return pl.pallas_call(
        functools.partial(_win_kernel, width=width, nblocks=nblocks, unroll=unroll),
        grid=(N, npair // pp),
        in_specs=[pl.BlockSpec(memory_space=pltpu.SMEM),
                  pl.BlockSpec((None, pp * PAIR_W, L), lambda n, p: (n, p, 0)),
                  pl.BlockSpec((None, L, pp * PAIR_W), lambda n, p: (n, 0, p)),
                  pl.BlockSpec((None, pp * PAIR_W, L), lambda n, p: (n, p, 0)),
                  pl.BlockSpec((nvar, pp, width, 2 * WIN_TQ), lambda n, p: (0, p, 0, 0))],
        out_specs=out_specs,
        out_shape=out_shape,
        compiler_params=_cparams("parallel", "parallel"),
        name=f"windowed_w{width}",
    )(tab, qt, k, vt, bias)


def _bmerge_kernel(o0_ref, l0_ref, o1_ref, l1_ref, o2_ref, l2_ref, y_ref, so1, sl1, so2, sl2, *, dils):
    T, width = y_ref.shape
    nchunk = width // LANES
    for (o_ref, l_ref, so, sl, dil) in ((o1_ref, l1_ref, so1, sl1, dils[1]), (o2_ref, l2_ref, so2, sl2, dils[2])):
        sub = T // dil
        for j in range(dil):
            for c in range(nchunk):
                so[c, pl.ds(j, sub, stride=dil), :] = o_ref[j, :, c * LANES:(c + 1) * LANES]
                sl[c, pl.ds(j, sub, stride=dil), :] = l_ref[j, :, c * LANES:(c + 1) * LANES]
    for c in range(nchunk):
        cols = slice(c * LANES, (c + 1) * LANES)
        o0, e0 = o0_ref[0, :, cols], l0_ref[0, :, cols]
        e1, e2 = sl1[c], sl2[c]
        mx = jnp.maximum(jnp.maximum(e0, e1), e2)
        w0, w1, w2 = jnp.exp2(e0 - mx), jnp.exp2(e1 - mx), jnp.exp2(e2 - mx)
        num = w0 * o0 + w1 * so1[c] + w2 * so2[c]
        y_ref[:, cols] = (num / (w0 + w1 + w2)).astype(y_ref.dtype)


def _bmerge(outs, S):
    B = outs[0][0].shape[0]
    dils = tuple(d for _, d in B_PATTERNS)
    T = min(512, S)
    args, in_specs = [], []
    for (o, e), dil in zip(outs, dils):
        spec = pl.BlockSpec((None, dil, T // dil, BRANCH_W), lambda b, t: (b, 0, t, 0))
        args += [o, e]
        in_specs += [spec, spec]
    return pl.pallas_call(
        functools.partial(_bmerge_kernel, dils=dils),
        grid=(B, S // T),
        in_specs=in_specs,
        out_specs=pl.BlockSpec((None, T, BRANCH_W), lambda b, t: (b, t, 0)),
        out_shape=jax.ShapeDtypeStruct((B, S, BRANCH_W), BF16),
        scratch_shapes=[pltpu.VMEM((BRANCH_W // LANES, T, LANES), F32)] * 4,
        compiler_params=_cparams("parallel", "parallel"),
        name="dilated_merge",
    )(*args)


def _merge_kernel(x_ref, nw_ref, ya_ref, yb_ref, yc_ref, yd_ref, wg_ref, bg_ref, wb_ref, wo_ref, o_ref):
    x = x_ref[...]
    h = _rms(x, nw_ref[...]).astype(BF16)
    merged = None
    for n, y_ref in enumerate((ya_ref, yb_ref, yc_ref, yd_ref)):
        cols = slice(n * D_MODEL, (n + 1) * D_MODEL)
        z = jnp.dot(h, wg_ref[:, cols], preferred_element_type=F32) + bg_ref[:, cols]
        gate = 1.0 / (1.0 + jnp.exp(-z))
        t = gate * jnp.dot(y_ref[...], wb_ref[n], preferred_element_type=F32)
        merged = t if merged is None else merged + t
    o_ref[...] = x + jnp.dot(merged.astype(BF16), wo_ref[...], preferred_element_type=F32)


def _merge(x2d, nw, ys, wg, bg, wb, wo, tm):
    Mrows, D = x2d.shape
    tm = min(tm, Mrows)
    row = lambda w: pl.BlockSpec((tm, w), lambda i: (i, 0))
    return pl.pallas_call(
        _merge_kernel,
        grid=(Mrows // tm,),
        in_specs=[row(D), _const_spec((1, D)), row(BRANCH_W), row(BRANCH_W), row(BRANCH_W), row(BRANCH_W),
                  _const_spec(wg.shape), _const_spec(bg.shape), _const_spec(wb.shape), _const_spec(wo.shape)],
        out_specs=row(D),
        out_shape=jax.ShapeDtypeStruct((Mrows, D), F32),
        compiler_params=_cparams("parallel"),
        name="gated_merge",
    )(x2d, nw, *ys, wg, bg, wb, wo)


FFN_CHUNK = 1024


def _ffn_kernel(x_ref, nw_ref, w1_ref, w2_ref, nf_ref, o_ref, *, final):
    x = x_ref[...]
    h = _rms(x, nw_ref[...]).astype(BF16)
    acc = x
    for c0 in range(0, D_FF, FFN_CHUNK):
        u = jnp.maximum(jnp.dot(h, w1_ref[:, c0:c0 + FFN_CHUNK], preferred_element_type=F32), 0.0)
        acc = acc + jnp.dot((u * u).astype(BF16), w2_ref[c0:c0 + FFN_CHUNK, :], preferred_element_type=F32)
    if final:
        acc = _rms(acc, nf_ref[...])
    o_ref[...] = acc


def _ffn(x2d, nw, w1, w2, nf, final, tm):
    Mrows, D = x2d.shape
    tm = min(tm, Mrows)
    row = pl.BlockSpec((tm, D), lambda i: (i, 0))
    return pl.pallas_call(
        functools.partial(_ffn_kernel, final=final),
        grid=(Mrows // tm,),
        in_specs=[row, _const_spec((1, D)), _const_spec(w1.shape), _const_spec(w2.shape), _const_spec((1, D))],
        out_specs=row,
        out_shape=jax.ShapeDtypeStruct((Mrows, D), F32),
        compiler_params=_cparams("parallel"),
        name="relu2_mlp",
    )(x2d, nw, w1, w2, nf)


def _alibi_slopes(n):
    return np.array([2.0 ** (-8.0 * (i + 1) / n) for i in range(n)], dtype=np.float32)


def _pair_tiles(t):
    nvar, nh, w, tq = t.shape
    t = t.reshape(nvar, nh // 2, 2, w, tq)
    return jnp.concatenate([t[:, :, 0], t[:, :, 1]], axis=-1)


def _dilated_tables(L, dil):
    width = 3 * WIN_TQ
    nb = L // WIN_TQ
    ws = np.clip((np.arange(nb) - 1) * WIN_TQ, 0, L - width)
    var = np.where(np.arange(nb) == 0, 0, np.where(np.arange(nb) == nb - 1, 2, 1))
    tab = np.stack([ws, var], axis=1).astype(np.int32)
    c = np.arange(width)[:, None]
    r = np.arange(WIN_TQ)[None, :]
    slopes = _alibi_slopes(B_HEADS)
    tiles = []
    for shift in (0, WIN_TQ, 2 * WIN_TQ):
        rel = c - shift - r
        valid = np.abs(rel) <= B_RADIUS
        bias = -slopes[:, None, None] * (np.abs(rel) * dil).astype(np.float32)[None] * np.float32(LOG2E)
        tiles.append(np.where(valid[None], bias, np.float32(NEG_INF)))
    return jnp.asarray(tab), _pair_tiles(jnp.asarray(np.stack(tiles).astype(np.float32)))


def _na_tables(S, rpb):
    rows = S // GRID_W
    kh = min(NA_KH, rows)
    wrows = 10
    nb = rows // 2
    r0s = 2 * np.arange(nb)
    wsr = np.clip(r0s - kh // 2, 0, rows - wrows)
    tab = np.stack([wsr * GRID_W, (r0s - wsr) // 2], axis=1).astype(np.int32)
    reps = [0, 2, 4, rows - 4, rows - 2]
    krl = np.arange(wrows)
    kc = np.arange(GRID_W)
    qrl = np.arange(2)
    qc = np.arange(GRID_W)
    qstart = np.clip(qc - NA_KW // 2, 0, GRID_W - NA_KW)
    col_ok = (kc[:, None] >= qstart[None, :]) & (kc[:, None] < qstart[None, :] + NA_KW)
    ridx, valid = [], []
    for r0 in reps:
        w0 = int(np.clip(r0 - kh // 2, 0, rows - wrows))
        r = (r0 + qrl)[None, :]
        rs = np.clip(r - kh // 2, 0, rows - kh)
        kr = (w0 + krl)[:, None]
        row_ok = (kr >= rs) & (kr < rs + kh)
        ridx.append(np.clip(kr - r + NA_KH - 1, 0, 2 * NA_KH - 2))
        valid.append(row_ok[:, None, :, None] & col_ok[None, :, None, :])
    ridx, valid = np.stack(ridx), np.stack(valid)
    nh = rpb.shape[0]
    span = GRID_W - NA_KW
    padded = jnp.pad(rpb.astype(F32), ((0, 0), (0, 0), (span, span)), mode="edge")
    by_col = jnp.stack([padded[:, :, GRID_W - 1 - c:2 * GRID_W - 1 - c] for c in range(GRID_W)], axis=-1)
    slabs = jnp.take(by_col, jnp.asarray(ridx.reshape(-1)), axis=1)
    slabs = slabs.reshape(nh, len(reps), wrows, 2, GRID_W, GRID_W).transpose(1, 0, 2, 4, 3, 5)
    tiles = jnp.where(valid[:, None], slabs * LOG2E, NEG_INF)
    tiles = tiles.reshape(len(reps), nh, wrows * GRID_W, 2 * GRID_W)
    return jnp.asarray(tab), _pair_tiles(tiles)


def _rope_tables(S):
    n = HEAD_DIM // 4
    pos = np.arange(S)
    inv = ROPE_THETA ** (-np.arange(n, dtype=np.float32) / n)
    d = np.arange(HEAD_DIM)
    p = np.where((d < HEAD_DIM // 2)[:, None], (pos // GRID_W)[None, :], (pos % GRID_W)[None, :]).astype(np.float32)
    ang = p * inv[d % n][:, None]
    sign = np.where((d % (2 * n)) < n, -1.0, 1.0).astype(np.float32)[:, None]
    return jnp.asarray(np.cos(ang).astype(np.float32)), jnp.asarray((np.sin(ang) * sign).astype(np.float32))


def kernel(x, norm_mix, w_in, b_gate, diff_lambda, diff_subln, na_rpb, qk_norm, w_branch, w_out, norm_ffn,
           w_ff1, w_ff2, norm_final):
    B, S, D = x.shape
    depth = w_in.shape[0]
    rows = B * S

    ones_col = jnp.ones((1, 3 * BRANCH_W), F32)
    qscale_col = ones_col.at[:, 0:BRANCH_W].set(QK_SCALE * LOG2E)
    a_slopes = jnp.asarray(_alibi_slopes(A_HEADS))
    cos_t, sg_t = _rope_tables(S)
    b_tabs = [_dilated_tables(S // dil, dil) for _, dil in B_PATTERNS]
    d_perm = np.array([(g * 4 + c) * HEAD_DIM + d for c in range(4) for g in range(D_KV_HEADS)
                       for d in range(HEAD_DIM)])

    o_a, o_b, o_c, o_d, o_gate = 0, 1536, 6144, 7680, 8448
    for l in range(depth):
        w = w_in[l].astype(BF16)
        nw = norm_mix[l].reshape(1, D)
        lam_init = 0.8 - 0.6 * math.exp(-0.3 * l)

        aq, ak, av = _project(x, nw, w[:, o_a:o_a + 1536], qscale_col, 1, v_ones=True)
        y_a = _flash_a(a_slopes, diff_lambda[l], diff_subln[l].reshape(PAIR_W, 1),
                       aq[:, 0], ak[:, 0], av[:, 0], lam_init, 512, 1024)

        b_outs = []
        for g, (_, dil) in enumerate(B_PATTERNS):
            lo = o_b + g * 1536
            bq, bk, bv = _project(x, nw, w[:, lo:lo + 1536], qscale_col, dil)
            L = S // dil
            tab, bias = b_tabs[g]
            o, e = _windowed(tab, bq.reshape(B * dil, BRANCH_W, L), bk.reshape(B * dil, L, BRANCH_W),
                             bv.reshape(B * dil, BRANCH_W, L), bias, F32, True)
            b_outs.append((o.reshape(B, dil, L, BRANCH_W), e.reshape(B, dil, L, BRANCH_W)))
        y_b = _bmerge(b_outs, S)

        cq, ck, cv = _project(x, nw, w[:, o_c:o_c + 1536], qscale_col, 1)
        c_tab, c_bias = _na_tables(S, na_rpb[l])
        (y_c,) = _windowed(c_tab, cq[:, 0], ck[:, 0], cv[:, 0], c_bias, BF16, False)

        wd = jnp.concatenate([w[:, o_d:o_d + 512][:, d_perm], w[:, o_d + 512:o_gate]], axis=1)
        dq, dk, dv = _project_d(x, nw, wd, qk_norm[l, 0].reshape(HEAD_DIM, 1), qk_norm[l, 1].reshape(HEAD_DIM, 1),
                                cos_t, sg_t)
        y_d = _flash_d(dq, dk, dv, 128, 1024)

        wb = w_branch[l].astype(BF16)
        wb = wb.at[3].set(wb[3][d_perm, :])
        ys = [y.reshape(rows, BRANCH_W) for y in (y_a, y_b, y_c, y_d)]
        x2d = _merge(x.reshape(rows, D), nw, ys, w[:, o_gate:], b_gate[l].reshape(1, -1), wb,
                     w_out[l].astype(BF16), 512)

        x2d = _ffn(x2d, norm_ffn[l].reshape(1, D), w_ff1[l].astype(BF16), w_ff2[l].astype(BF16),
                   norm_final.reshape(1, D), l == depth - 1, 512)
        x = x2d.reshape(B, S, D)
    return x
```

```python
import functools
import math

import numpy as np
import jax
import jax.numpy as jnp
from jax import lax
from jax.experimental import pallas as pl
from jax.experimental.pallas import tpu as pltpu

F32 = jnp.float32
BF16 = jnp.bfloat16

D_MODEL = 1024
GRID_W = 64
HEAD_DIM = 64
PAIR_W = 2 * HEAD_DIM
N_BRANCH = 4
BRANCH_W = 512
A_HEADS = 4
B_PATTERNS = ((128, 1), (512, 4), (2048, 16))
B_HEADS = 8
B_RADIUS = 64
C_HEADS = 8
NA_KH = 8
NA_KW = 16
D_HEADS = 8
D_KV_HEADS = 2
ROPE_THETA = 10000.0
D_FF = 4 * D_MODEL
EPS = 1e-6
NEG_INF = -1e30
LOG2E = math.log2(math.e)
QK_SCALE = HEAD_DIM ** -0.5

VMEM_LIMIT_BYTES = 56 * 1024 * 1024


def _cparams(*sem):
    return pltpu.CompilerParams(dimension_semantics=sem, vmem_limit_bytes=VMEM_LIMIT_BYTES)


def _const_spec(shape):
    nd = len(shape)
    return pl.BlockSpec(shape, lambda *_: (0,) * nd)


def _rms(x, g):
    return x * lax.rsqrt(jnp.mean(x * x, axis=-1, keepdims=True) + EPS) * g


LANES = 128


def _normed_rows(x_ref, nw_ref, hs_ref, xs_ref, dil):
    rows, width = x_ref.shape
    sub = rows // dil
    h = _rms(x_ref[...], nw_ref[...])
    if dil == 1:
        hs_ref[...] = h.astype(BF16)
        return
    for c in range(width // LANES):
        xs_ref[c] = h[:, c * LANES:(c + 1) * LANES]
    for j in range(dil):
        for c in range(width // LANES):
            hs_ref[j * sub:(j + 1) * sub, c * LANES:(c + 1) * LANES] = (
                xs_ref[c, pl.ds(j, sub, stride=dil), :].astype(BF16))


ONES_ROWS = 16
VT_ROWS = PAIR_W + ONES_ROWS


def _proj_kernel(x_ref, nw_ref, w_ref, cs_ref, q_ref, k_ref, v_ref, hs_ref, xs_ref, *, dil, v_ones):
    _normed_rows(x_ref, nw_ref, hs_ref, xs_ref, dil)
    sub = x_ref.shape[0] // dil
    h = hs_ref[...]
    for idx, (o_ref, transposed) in enumerate(((q_ref, True), (k_ref, False), (v_ref, True))):
        lo = idx * BRANCH_W
        acc = jnp.dot(h, w_ref[:, lo:lo + BRANCH_W], preferred_element_type=F32)
        acc = acc * cs_ref[:, lo:lo + BRANCH_W]
        for j in range(dil):
            part = acc[j * sub:(j + 1) * sub, :]
            if o_ref is v_ref and v_ones:
                pt = part.T.astype(BF16)
                for p in range(BRANCH_W // PAIR_W):
                    o_ref[j, p * VT_ROWS:p * VT_ROWS + PAIR_W, :] = pt[p * PAIR_W:(p + 1) * PAIR_W, :]
                    o_ref[j, p * VT_ROWS + PAIR_W:(p + 1) * VT_ROWS, :] = jnp.ones((ONES_ROWS, sub), BF16)
            else:
                o_ref[j] = (part.T if transposed else part).astype(o_ref.dtype)


def _project(x, nw, w, colscale, dil, v_ones=False):
    B, S, D = x.shape
    L = S // dil
    R = max(1024, 128 * dil)
    R = min(R, S)
    sub = R // dil
    N = w.shape[1]
    vrows = (BRANCH_W // PAIR_W) * VT_ROWS if v_ones else BRANCH_W
    t_shape = jax.ShapeDtypeStruct((B, dil, BRANCH_W, L), BF16)
    n_shape = jax.ShapeDtypeStruct((B, dil, L, BRANCH_W), BF16)
    v_shape = jax.ShapeDtypeStruct((B, dil, vrows, L), BF16)
    t_spec = pl.BlockSpec((None, dil, BRANCH_W, sub), lambda b, i: (b, 0, 0, i))
    n_spec = pl.BlockSpec((None, dil, sub, BRANCH_W), lambda b, i: (b, 0, i, 0))
    v_spec = pl.BlockSpec((None, dil, vrows, sub), lambda b, i: (b, 0, 0, i))
    return pl.pallas_call(
        functools.partial(_proj_kernel, dil=dil, v_ones=v_ones),
        grid=(B, S // R),
        in_specs=[pl.BlockSpec((None, R, D), lambda b, i: (b, i, 0)),
                  _const_spec((1, D)), _const_spec((D, N)), _const_spec((1, N))],
        out_specs=[t_spec, n_spec, v_spec],
        out_shape=[t_shape, n_shape, v_shape],
        scratch_shapes=[pltpu.VMEM((R, D), BF16),
                        pltpu.VMEM((D // LANES, R if dil > 1 else 8, LANES), F32)],
        compiler_params=_cparams("parallel", "parallel"),
        name=f"proj_dil{dil}",
    )(x, nw, w, colscale)


def _norm_rope_t(yt, g, cos, sg):
    outs = []
    for h in range(yt.shape[0] // HEAD_DIM):
        yh = yt[h * HEAD_DIM:(h + 1) * HEAD_DIM, :]
        yn = yh * lax.rsqrt(jnp.mean(yh * yh, axis=0, keepdims=True) + EPS) * g
        swapped = jnp.concatenate([yn[16:32], yn[0:16], yn[48:64], yn[32:48]], axis=0)
        outs.append(yn * cos + swapped * sg)
    return jnp.concatenate(outs, axis=0)


def _proj_d_kernel(x_ref, nw_ref, w_ref, gq_ref, gk_ref, cos_ref, sg_ref, q_ref, k_ref, v_ref, hs_ref):
    _normed_rows(x_ref, nw_ref, hs_ref, None, 1)
    h = hs_ref[...]
    cos = cos_ref[...]
    sg = sg_ref[...]
    qw = D_HEADS * HEAD_DIM
    kw = D_KV_HEADS * HEAD_DIM
    qt = jnp.dot(h, w_ref[:, 0:qw], preferred_element_type=F32).T
    q_ref[...] = (_norm_rope_t(qt, gq_ref[...], cos, sg) * (QK_SCALE * LOG2E)).astype(BF16)
    kt = jnp.dot(h, w_ref[:, qw:qw + kw], preferred_element_type=F32).T
    k_ref[...] = _norm_rope_t(kt, gk_ref[...], cos, sg).T.astype(BF16)
    v_ref[0:kw, :] = jnp.dot(h, w_ref[:, qw + kw:qw + 2 * kw], preferred_element_type=F32).T.astype(BF16)
    v_ref[kw:, :] = jnp.ones((ONES_ROWS, v_ref.shape[1]), BF16)


def _project_d(x, nw, w, gq, gk, cos_t, sg_t):
    B, S, D = x.shape
    R = min(1024, S)
    qw = D_HEADS * HEAD_DIM
    kw = D_KV_HEADS * HEAD_DIM
    return pl.pallas_call(
        _proj_d_kernel,
        grid=(B, S // R),
        in_specs=[pl.BlockSpec((None, R, D), lambda b, i: (b, i, 0)),
                  _const_spec((1, D)), _const_spec((D, qw + 2 * kw)),
                  _const_spec((HEAD_DIM, 1)), _const_spec((HEAD_DIM, 1)),
                  pl.BlockSpec((HEAD_DIM, R), lambda b, i: (0, i)),
                  pl.BlockSpec((HEAD_DIM, R), lambda b, i: (0, i))],
        out_specs=[pl.BlockSpec((None, qw, R), lambda b, i: (b, 0, i)),
                   pl.BlockSpec((None, R, kw), lambda b, i: (b, i, 0)),
                   pl.BlockSpec((None, VT_ROWS, R), lambda b, i: (b, 0, i))],
        out_shape=[jax.ShapeDtypeStruct((B, qw, S), BF16),
                   jax.ShapeDtypeStruct((B, S, kw), BF16),
                   jax.ShapeDtypeStruct((B, VT_ROWS, S), BF16)],
        scratch_shapes=[pltpu.VMEM((R, D), BF16)],
        compiler_params=_cparams("parallel", "parallel"),
        name="proj_gqa",
    )(x, nw, w, gq, gk, cos_t, sg_t)


FLASH_COLS = 256


def _group_spans(c0, width):
    spans, c = [], c0
    while c < c0 + width:
        g, lo = divmod(c, FLASH_COLS)
        n = min(FLASH_COLS - lo, c0 + width - c)
        spans.append((g, lo, n))
        c += n
    return spans


def _read_cols(ref, rows, c0, width):
    parts = [ref[g, rows, lo:lo + n] for g, lo, n in _group_spans(c0, width)]
    return parts[0] if len(parts) == 1 else jnp.concatenate(parts, axis=1)


def _stack_pair_queries(q2_ref, qt_ref, nchunk, tq):
    row = lax.broadcasted_iota(jnp.int32, (PAIR_W, tq), 0)
    for c in range(nchunk):
        qc = qt_ref[c * PAIR_W:(c + 1) * PAIR_W, :].astype(F32)
        for half, keep in enumerate((row < HEAD_DIM, row >= HEAD_DIM)):
            val = jnp.where(keep, qc, 0.0).astype(BF16)
            off = 0
            for g, lo, n in _group_spans((2 * c + half) * tq, tq):
                q2_ref[g, :, lo:lo + n] = val[:, off:off + n]
                off += n


def _flash_scratch(M, tk):
    G = M // FLASH_COLS
    stat = pltpu.VMEM((G, 1, FLASH_COLS), F32)
    probs = pltpu.VMEM((G, tk, FLASH_COLS), BF16)
    return [pltpu.VMEM((G, PAIR_W, FLASH_COLS), BF16), stat, stat,
            pltpu.VMEM((G, VT_ROWS, FLASH_COLS), F32), probs,
            pltpu.VMEM((G, tk, FLASH_COLS), F32), pltpu.VMEM((G, tk, FLASH_COLS), F32), stat, stat,
            probs, stat, stat]


def _flash_loop(k_ref, vt_ref, q2_ref, m_ref, a_ref, acc_ref, p_ref, sa_ref, sb_ref, ma_ref, mb_ref,
                *, tq, tk, seq, bias_fn):
    ngroups = q2_ref.shape[0]
    nk = seq // tk
    m_ref[...] = jnp.full(m_ref.shape, NEG_INF, F32)
    acc_ref[...] = jnp.zeros(acc_ref.shape, F32)

    def scores(j, g, s_ref, mx_ref):
        k = k_ref[pl.ds(pl.multiple_of(j * tk, tk), tk), :]
        s = jnp.dot(k, q2_ref[g], preferred_element_type=F32)
        if bias_fn is not None:
            tile, shift, _ = bias_fn(j, (g * FLASH_COLS) % tq)
            s = s + tile + shift
        s_ref[g] = s
        mx_ref[g] = jnp.max(s, axis=0, keepdims=True)

    def probs(g, s_ref, mx_ref):
        m_old = m_ref[g]
        m_new = jnp.maximum(m_old, mx_ref[g])
        m_ref[g] = m_new
        a_ref[g] = jnp.exp2(m_old - m_new)
        p_ref[g] = jnp.exp2((s_ref[g] - m_new).astype(BF16))

    def values(j, g):
        vt = vt_ref[:, pl.ds(pl.multiple_of(j * tk, tk), tk)]
        acc_ref[g] = acc_ref[g] * a_ref[g] + jnp.dot(vt, p_ref[g], preferred_element_type=F32)

    def step(j, cur, nxt):
        for g in range(ngroups):
            if nxt is not None:
                scores(j + 1, g, *nxt)
            probs(g, *cur)
            if g > 0:
                values(j, g - 1)
        values(j, ngroups - 1)

    buf_a, buf_b = (sa_ref, ma_ref), (sb_ref, mb_ref)
    for g in range(ngroups):
        scores(0, g, *buf_a)

    def body(i, carry):
        step(2 * i, buf_a, buf_b)
        step(2 * i + 1, buf_b, buf_a)
        return carry

    lax.fori_loop(0, nk // 2 - 1, body, 0)
    step(nk - 2, buf_a, buf_b)
    step(nk - 1, buf_b, None)


LAG_LIMIT = 64.0
FLASH_PROBE_ROWS = 128


def _flash_attend(k_ref, vt_ref, q2_ref, m_ref, r_ref, acc_ref, pa_ref, sa_ref, sb_ref, ra_ref, rb_ref,
                  pb_ref, lag_ref, mu_ref, *, tq, tk, seq, bias_fn):
    ngroups = q2_ref.shape[0]
    nk = seq // tk

    def block_scores(j, g, rows=tk):
        k = k_ref[pl.ds(pl.multiple_of(j * tk, tk), rows), :]
        s = jnp.dot(k, q2_ref[g], preferred_element_type=F32)
        if bias_fn is None:
            return s, 0.0, 0.0
        tile, shift, bias_max = bias_fn(j, (g * FLASH_COLS) % tq, rows)
        return s + tile, shift, bias_max

    for g in range(ngroups):
        s, shift, bias_max = block_scores(0, g, FLASH_PROBE_ROWS)
        first = jnp.max(s, axis=0, keepdims=True) + shift
        m_ref[g] = first
        mu_ref[g] = first - bias_max
        r_ref[g] = first
    acc_ref[...] = jnp.zeros(acc_ref.shape, F32)
    lag_ref[...] = jnp.zeros(lag_ref.shape, F32)

    def scores(j, g, p_ref, rp_ref):
        s, shift, bias_max = block_scores(j, g)
        m_old = m_ref[g]
        ref = jnp.maximum(m_old, mu_ref[g] + bias_max)
        rp_ref[g] = ref
        p_ref[g] = jnp.exp2((s - (ref - shift)).astype(BF16))
        mx = jnp.max(s, axis=0, keepdims=True) + shift
        m_new = jnp.maximum(m_old, mx)
        lag_ref[g] = jnp.maximum(lag_ref[g], jnp.maximum(mx - ref, ref - m_new))
        m_ref[g] = m_new
        mu_ref[g] = jnp.maximum(mu_ref[g], mx - bias_max)

    def values(j, g, p_ref, rp_ref):
        vt = vt_ref[:, pl.ds(pl.multiple_of(j * tk, tk), tk)]
        ref = rp_ref[g]
        acc_ref[g] = (acc_ref[g] * jnp.exp2(r_ref[g] - ref)
                      + jnp.dot(vt, p_ref[g], preferred_element_type=F32))
        r_ref[g] = ref

    def step(j, cur, nxt):
        for g in range(ngroups):
            if nxt is not None:
                scores(j + 1, g, *nxt)
            values(j, g, *cur)

    buf_a, buf_b = (pa_ref, ra_ref), (pb_ref, rb_ref)
    for g in range(ngroups):
        scores(0, g, *buf_a)

    def body(i, carry):
        step(2 * i, buf_a, buf_b)
        step(2 * i + 1, buf_b, buf_a)
        return carry

    lax.fori_loop(0, nk // 2 - 1, body, 0)
    step(nk - 2, buf_a, buf_b)
    step(nk - 1, buf_b, None)

    @pl.when(jnp.max(lag_ref[...]) > LAG_LIMIT)
    def _():
        _flash_loop(k_ref, vt_ref, q2_ref, m_ref, r_ref, acc_ref, pa_ref, sa_ref, sb_ref, ra_ref, rb_ref,
                    tq=tq, tk=tk, seq=seq, bias_fn=bias_fn)


def _flash_a_kernel(slope_ref, lam_ref, g_ref, qt_ref, k_ref, vt_ref, o_ref,
                    q2_ref, m_ref, a_ref, acc_ref, *bufs, tq, tk, seq, lam_init):
    _stack_pair_queries(q2_ref, qt_ref, 1, tq)
    *bufs, tiles_ref = bufs
    negc2 = -LOG2E * slope_ref[pl.program_id(1)]
    q0 = pl.program_id(2) * tq
    @pl.when(pl.program_id(2) == 0)
    def _():
        d0 = (lax.broadcasted_iota(jnp.int32, (tk, FLASH_COLS), 1)
              - lax.broadcasted_iota(jnp.int32, (tk, FLASH_COLS), 0)).astype(F32)
        tiles_ref[0] = d0 * negc2
        tiles_ref[1] = d0 * (-negc2)
        for n in range(tk // FLASH_COLS):
            tiles_ref[2 + n] = jnp.abs(d0 + float(n * FLASH_COLS)) * negc2

    qcol = lax.broadcasted_iota(jnp.int32, (1, FLASH_COLS), 1).astype(F32)

    def bias_fn(j, qcol0, rows=tk):
        delta = q0 + qcol0 - j * tk
        right, left = delta >= tk, delta <= -FLASH_COLS
        idx = jnp.where(right, 0, jnp.where(left, 1, 2 + jnp.clip(delta // FLASH_COLS, 0, tk // FLASH_COLS - 1)))
        shift = jnp.where(right | left, negc2 * jnp.abs(delta).astype(F32), 0.0)
        t = qcol + delta.astype(F32)
        nearest = jnp.maximum(jnp.maximum(t - float(rows - 1), -t), 0.0)
        return tiles_ref[idx, 0:rows, :], shift, negc2 * nearest

    _flash_attend(k_ref, vt_ref, q2_ref, m_ref, a_ref, acc_ref, *bufs,
                tq=tq, tk=tk, seq=seq, bias_fn=bias_fn)

    lp = lam_ref[...]
    lam = (jnp.exp(jnp.sum(lp[0:1] * lp[1:2], axis=1, keepdims=True))
           - jnp.exp(jnp.sum(lp[2:3] * lp[3:4], axis=1, keepdims=True)) + lam_init)
    vals, sums = slice(0, PAIR_W), slice(PAIR_W, PAIR_W + 1)
    o = (_read_cols(acc_ref, vals, 0, tq) / _read_cols(acc_ref, sums, 0, tq)
         - lam * (_read_cols(acc_ref, vals, tq, tq) / _read_cols(acc_ref, sums, tq, tq)))
    o = o * lax.rsqrt(jnp.mean(o * o, axis=0, keepdims=True) + EPS) * g_ref[...]
    o_ref[...] = (o * (1.0 - lam_init)).T.astype(o_ref.dtype)


def _flash_a(slopes, lamp, subg, qt, k, vt, lam_init, tq, tk):
    B, _, S = qt.shape
    tq, tk = min(tq, S), min(tk, S)
    M = 2 * tq
    return pl.pallas_call(
        functools.partial(_flash_a_kernel, tq=tq, tk=tk, seq=S, lam_init=lam_init),
        grid=(B, A_HEADS, S // tq),
        in_specs=[pl.BlockSpec(memory_space=pltpu.SMEM),
                  _const_spec((4, HEAD_DIM)), _const_spec((PAIR_W, 1)),
                  pl.BlockSpec((None, PAIR_W, tq), lambda b, h, i: (b, h, i)),
                  pl.BlockSpec((None, S, PAIR_W), lambda b, h, i: (b, 0, h)),
                  pl.BlockSpec((None, VT_ROWS, S), lambda b, h, i: (b, h, 0))],
        out_specs=pl.BlockSpec((None, tq, PAIR_W), lambda b, h, i: (b, i, h)),
        out_shape=jax.ShapeDtypeStruct((B, S, A_HEADS * PAIR_W), BF16),
        scratch_shapes=_flash_scratch(M, tk) + [pltpu.VMEM((2 + tk // FLASH_COLS, tk, FLASH_COLS), F32)],
        compiler_params=_cparams("parallel", "parallel", "arbitrary"),
        name="flash_diff",
    )(slopes, lamp, subg, qt, k, vt)


def _flash_d_kernel(qt_ref, k_ref, vt_ref, o_ref, q2_ref, m_ref, a_ref, acc_ref, *bufs, tq, tk, seq):
    nchunk = qt_ref.shape[0] // PAIR_W
    _stack_pair_queries(q2_ref, qt_ref, nchunk, tq)
    _flash_attend(k_ref, vt_ref, q2_ref, m_ref, a_ref, acc_ref, *bufs,
                tq=tq, tk=tk, seq=seq, bias_fn=None)
    sums = slice(PAIR_W, PAIR_W + 1)
    for c in range(nchunk):
        top, bot = (2 * c) * tq, (2 * c + 1) * tq
        ot = jnp.concatenate(
            [_read_cols(acc_ref, slice(0, HEAD_DIM), top, tq) / _read_cols(acc_ref, sums, top, tq),
             _read_cols(acc_ref, slice(HEAD_DIM, PAIR_W), bot, tq) / _read_cols(acc_ref, sums, bot, tq)],
            axis=0)
        o_ref[:, c * PAIR_W:(c + 1) * PAIR_W] = ot.T.astype(o_ref.dtype)


def _flash_d(qt, k, vt, tq, tk):
    B, qw, S = qt.shape
    tq, tk = min(tq, S), min(tk, S)
    M = 2 * (qw // PAIR_W) * tq
    return pl.pallas_call(
        functools.partial(_flash_d_kernel, tq=tq, tk=tk, seq=S),
        grid=(B, S // tq),
        in_specs=[pl.BlockSpec((None, qw, tq), lambda b, i: (b, 0, i)),
                  pl.BlockSpec((None, S, PAIR_W), lambda b, i: (b, 0, 0)),
                  pl.BlockSpec((None, VT_ROWS, S), lambda b, i: (b, 0, 0))],
        out_specs=pl.BlockSpec((None, tq, qw), lambda b, i: (b, i, 0)),
        out_shape=jax.ShapeDtypeStruct((B, S, qw), BF16),
        scratch_shapes=_flash_scratch(M, tk),
        compiler_params=_cparams("parallel", "parallel"),
        name="flash_gqa",
    )(qt, k, vt)


WIN_TQ = 128
WIN_UNROLL = 8


def _win_kernel(tab_ref, qt_ref, k_ref, vt_ref, bias_ref, o_ref, *lse_refs, width, nblocks, unroll):
    row = lax.broadcasted_iota(jnp.int32, (PAIR_W, WIN_TQ), 0)
    pairs = qt_ref.shape[0] // PAIR_W

    def scores(pi, i):
        lanes = slice(pi * PAIR_W, (pi + 1) * PAIR_W)
        ws = pl.multiple_of(tab_ref[i, 0], 128)
        q0 = pl.multiple_of(i * WIN_TQ, WIN_TQ)
        qc = qt_ref[lanes, pl.ds(q0, WIN_TQ)].astype(F32)
        q2 = jnp.concatenate([jnp.where(row < HEAD_DIM, qc, 0.0),
                              jnp.where(row >= HEAD_DIM, qc, 0.0)], axis=1).astype(BF16)
        s = jnp.dot(k_ref[pl.ds(ws, width), lanes], q2, preferred_element_type=F32)
        return pi, lanes, ws, q0, s + bias_ref[tab_ref[i, 1], pi]

    def finish(lanes, q0, m, acc):
        l = acc[PAIR_W:PAIR_W + 1, :]
        rl = 1.0 / l
        ot = jnp.concatenate([acc[0:HEAD_DIM, 0:WIN_TQ] * rl[:, 0:WIN_TQ],
                              acc[HEAD_DIM:PAIR_W, WIN_TQ:] * rl[:, WIN_TQ:]], axis=0)
        o_ref[pl.ds(q0, WIN_TQ), lanes] = ot.T.astype(o_ref.dtype)
        if lse_refs:
            lse = m + jnp.log2(l)
            lt = jnp.concatenate([jnp.broadcast_to(lse[:, 0:WIN_TQ], (HEAD_DIM, WIN_TQ)),
                                  jnp.broadcast_to(lse[:, WIN_TQ:], (HEAD_DIM, WIN_TQ))], axis=0)
            lse_refs[0][pl.ds(q0, WIN_TQ), lanes] = lt.T

    def body(ib, carry):
        blocks = [scores(pi, ib * unroll + u) for pi in range(pairs) for u in range(unroll)]
        maxes, probs, accs = [], [], []

        def values(n):
            pi, ws = blocks[n][0], blocks[n][2]
            vt = vt_ref[pi * VT_ROWS:(pi + 1) * VT_ROWS, pl.ds(ws, width)]
            accs.append(jnp.dot(vt, probs[n], preferred_element_type=F32))

        for n, (*_, s) in enumerate(blocks):
            m = jnp.max(s, axis=0, keepdims=True)
            maxes.append(m)
            probs.append(jnp.exp2((s - m).astype(BF16)))
            if n > 0:
                values(n - 1)
        values(len(blocks) - 1)
        for (_, lanes, _, q0, _), m, acc in zip(blocks, maxes, accs):
            finish(lanes, q0, m, acc)
        return carry

    lax.fori_loop(0, nblocks // unroll, body, 0)


def _windowed(tab, qt, k, vt, bias, out_dtype, want_lse):
    N, _, L = qt.shape
    nvar, npair, width, _ = bias.shape
    nblocks = L // WIN_TQ
    unroll = min(WIN_UNROLL, nblocks)
    pp = min(npair, WIN_UNROLL // unroll)
    o_spec = pl.BlockSpec((None, L, pp * PAIR_W), lambda n, p: (n, 0, p))
    out_shape = [jax.ShapeDtypeStruct((N, L, npair * PAIR_W), out_dtype)]
    out_specs = [o_spec]
    if want_lse:
        out_shape.append(jax.ShapeDtypeStruct((N, L, npair * PAIR_W), F32))
        out_specs.append(o_spec)
    return pl.pallas_call(
        functools.partial(_win_kernel, width=width, nblocks=nblocks, unroll=unroll),
        grid=(N, npair // pp),
        in_specs=[pl.BlockSpec(memory_space=pltpu.SMEM),
                  pl.BlockSpec((None, pp * PAIR_W, L), lambda n, p: (n, p, 0)),
                  pl.BlockSpec((None, L, pp * PAIR_W), lambda n, p: (n, 0, p)),
                  pl.BlockSpec((None, pp * VT_ROWS, L), lambda n, p: (n, p, 0)),
                  pl.BlockSpec((nvar, pp, width, 2 * WIN_TQ), lambda n, p: (0, p, 0, 0))],
        out_specs=out_specs,
        out_shape=out_shape,
        compiler_params=_cparams("parallel", "parallel"),
        name=f"windowed_w{width}",
    )(tab, qt, k, vt, bias)


def _bmerge_kernel(o0_ref, l0_ref, o1_ref, l1_ref, o2_ref, l2_ref, y_ref, so1, sl1, so2, sl2, *, dils):
    T, width = y_ref.shape
    nchunk = width // LANES
    for (o_ref, l_ref, so, sl, dil) in ((o1_ref, l1_ref, so1, sl1, dils[1]), (o2_ref, l2_ref, so2, sl2, dils[2])):
        sub = T // dil
        for j in range(dil):
            for c in range(nchunk):
                so[c, pl.ds(j, sub, stride=dil), :] = o_ref[j, :, c * LANES:(c + 1) * LANES]
                sl[c, pl.ds(j, sub, stride=dil), :] = l_ref[j, :, c * LANES:(c + 1) * LANES]
    for c in range(nchunk):
        cols = slice(c * LANES, (c + 1) * LANES)
        o0, e0 = o0_ref[0, :, cols], l0_ref[0, :, cols]
        e1, e2 = sl1[c], sl2[c]
        mx = jnp.maximum(jnp.maximum(e0, e1), e2)
        w0, w1, w2 = jnp.exp2(e0 - mx), jnp.exp2(e1 - mx), jnp.exp2(e2 - mx)
        num = w0 * o0 + w1 * so1[c] + w2 * so2[c]
        y_ref[:, cols] = (num / (w0 + w1 + w2)).astype(y_ref.dtype)


def _bmerge(outs, S):
    B = outs[0][0].shape[0]
    dils = tuple(d for _, d in B_PATTERNS)
    T = min(512, S)
    args, in_specs = [], []
    for (o, e), dil in zip(outs, dils):
        spec = pl.BlockSpec((None, dil, T // dil, BRANCH_W), lambda b, t: (b, 0, t, 0))
        args += [o, e]
        in_specs += [spec, spec]
    return pl.pallas_call(
        functools.partial(_bmerge_kernel, dils=dils),
        grid=(B, S // T),
        in_specs=in_specs,
        out_specs=pl.BlockSpec((None, T, BRANCH_W), lambda b, t: (b, t, 0)),
        out_shape=jax.ShapeDtypeStruct((B, S, BRANCH_W), BF16),
        scratch_shapes=[pltpu.VMEM((BRANCH_W // LANES, T, LANES), F32)] * 4,
        compiler_params=_cparams("parallel", "parallel"),
        name="dilated_merge",
    )(*args)


def _merge_kernel(x_ref, nw_ref, ya_ref, yb_ref, yc_ref, yd_ref, wg_ref, bg_ref, wb_ref, wo_ref, o_ref):
    x = x_ref[...]
    h = _rms(x, nw_ref[...]).astype(BF16)
    merged = None
    for n, y_ref in enumerate((ya_ref, yb_ref, yc_ref, yd_ref)):
        cols = slice(n * D_MODEL, (n + 1) * D_MODEL)
        z = jnp.dot(h, wg_ref[:, cols], preferred_element_type=F32) + bg_ref[:, cols]
        gate = 1.0 / (1.0 + jnp.exp(-z))
        t = gate * jnp.dot(y_ref[...], wb_ref[n], preferred_element_type=F32)
        merged = t if merged is None else merged + t
    o_ref[...] = x + jnp.dot(merged.astype(BF16), wo_ref[...], preferred_element_type=F32)


def _merge(x2d, nw, ys, wg, bg, wb, wo, tm):
    Mrows, D = x2d.shape
    tm = min(tm, Mrows)
    row = lambda w: pl.BlockSpec((tm, w), lambda i: (i, 0))
    return pl.pallas_call(
        _merge_kernel,
        grid=(Mrows // tm,),
        in_specs=[row(D), _const_spec((1, D)), row(BRANCH_W), row(BRANCH_W), row(BRANCH_W), row(BRANCH_W),
                  _const_spec(wg.shape), _const_spec(bg.shape), _const_spec(wb.shape), _const_spec(wo.shape)],
        out_specs=row(D),
        out_shape=jax.ShapeDtypeStruct((Mrows, D), F32),
        compiler_params=_cparams("parallel"),
        name="gated_merge",
    )(x2d, nw, *ys, wg, bg, wb, wo)


FFN_CHUNK = 1024


def _ffn_kernel(x_ref, nw_ref, w1_ref, w2_ref, nf_ref, o_ref, *, final):
    x = x_ref[...]
    h = _rms(x, nw_ref[...]).astype(BF16)
    acc = x
    for c0 in range(0, D_FF, FFN_CHUNK):
        u = jnp.maximum(jnp.dot(h, w1_ref[:, c0:c0 + FFN_CHUNK], preferred_element_type=F32), 0.0)
        acc = acc + jnp.dot((u * u).astype(BF16), w2_ref[c0:c0 + FFN_CHUNK, :], preferred_element_type=F32)
    if final:
        acc = _rms(acc, nf_ref[...])
    o_ref[...] = acc


def _ffn(x2d, nw, w1, w2, nf, final, tm):
    Mrows, D = x2d.shape
    tm = min(tm, Mrows)
    row = pl.BlockSpec((tm, D), lambda i: (i, 0))
    return pl.pallas_call(
        functools.partial(_ffn_kernel, final=final),
        grid=(Mrows // tm,),
        in_specs=[row, _const_spec((1, D)), _const_spec(w1.shape), _const_spec(w2.shape), _const_spec((1, D))],
        out_specs=row,
        out_shape=jax.ShapeDtypeStruct((Mrows, D), F32),
        compiler_params=_cparams("parallel"),
        name="relu2_mlp",
    )(x2d, nw, w1, w2, nf)


def _alibi_slopes(n):
    return np.array([2.0 ** (-8.0 * (i + 1) / n) for i in range(n)], dtype=np.float32)


def _pair_tiles(t):
    nvar, nh, w, tq = t.shape
    t = t.reshape(nvar, nh // 2, 2, w, tq)
    return jnp.concatenate([t[:, :, 0], t[:, :, 1]], axis=-1)


def _dilated_tables(L, dil):
    width = 3 * WIN_TQ
    nb = L // WIN_TQ
    ws = np.clip((np.arange(nb) - 1) * WIN_TQ, 0, L - width)
    var = np.where(np.arange(nb) == 0, 0, np.where(np.arange(nb) == nb - 1, 2, 1))
    tab = np.stack([ws, var], axis=1).astype(np.int32)
    c = np.arange(width)[:, None]
    r = np.arange(WIN_TQ)[None, :]
    slopes = _alibi_slopes(B_HEADS)
    tiles = []
    for shift in (0, WIN_TQ, 2 * WIN_TQ):
        rel = c - shift - r
        valid = np.abs(rel) <= B_RADIUS
        bias = -slopes[:, None, None] * (np.abs(rel) * dil).astype(np.float32)[None] * np.float32(LOG2E)
        tiles.append(np.where(valid[None], bias, np.float32(NEG_INF)))
    return jnp.asarray(tab), _pair_tiles(jnp.asarray(np.stack(tiles).astype(np.float32)))


def _na_tables(S, rpb):
    rows = S // GRID_W
    kh = min(NA_KH, rows)
    wrows = 10
    nb = rows // 2
    r0s = 2 * np.arange(nb)
    wsr = np.clip(r0s - kh // 2, 0, rows - wrows)
    tab = np.stack([wsr * GRID_W, (r0s - wsr) // 2], axis=1).astype(np.int32)
    reps = [0, 2, 4, rows - 4, rows - 2]
    krl = np.arange(wrows)
    kc = np.arange(GRID_W)
    qrl = np.arange(2)
    qc = np.arange(GRID_W)
    qstart = np.clip(qc - NA_KW // 2, 0, GRID_W - NA_KW)
    col_ok = (kc[:, None] >= qstart[None, :]) & (kc[:, None] < qstart[None, :] + NA_KW)
    ridx, valid = [], []
    for r0 in reps:
        w0 = int(np.clip(r0 - kh // 2, 0, rows - wrows))
        r = (r0 + qrl)[None, :]
        rs = np.clip(r - kh // 2, 0, rows - kh)
        kr = (w0 + krl)[:, None]
        row_ok = (kr >= rs) & (kr < rs + kh)
        ridx.append(np.clip(kr - r + NA_KH - 1, 0, 2 * NA_KH - 2))
        valid.append(row_ok[:, None, :, None] & col_ok[None, :, None, :])
    ridx, valid = np.stack(ridx), np.stack(valid)
    nh = rpb.shape[0]
    span = GRID_W - NA_KW
    padded = jnp.pad(rpb.astype(F32), ((0, 0), (0, 0), (span, span)), mode="edge")
    by_col = jnp.stack([padded[:, :, GRID_W - 1 - c:2 * GRID_W - 1 - c] for c in range(GRID_W)], axis=-1)
    slabs = jnp.take(by_col, jnp.asarray(ridx.reshape(-1)), axis=1)
    slabs = slabs.reshape(nh, len(reps), wrows, 2, GRID_W, GRID_W).transpose(1, 0, 2, 4, 3, 5)
    tiles = jnp.where(valid[:, None], slabs * LOG2E, NEG_INF)
    tiles = tiles.reshape(len(reps), nh, wrows * GRID_W, 2 * GRID_W)
    return jnp.asarray(tab), _pair_tiles(tiles)


def _rope_tables(S):
    n = HEAD_DIM // 4
    pos = np.arange(S)
    inv = ROPE_THETA ** (-np.arange(n, dtype=np.float32) / n)
    d = np.arange(HEAD_DIM)
    p = np.where((d < HEAD_DIM // 2)[:, None], (pos // GRID_W)[None, :], (pos % GRID_W)[None, :]).astype(np.float32)
    ang = p * inv[d % n][:, None]
    sign = np.where((d % (2 * n)) < n, -1.0, 1.0).astype(np.float32)[:, None]
    return jnp.asarray(np.cos(ang).astype(np.float32)), jnp.asarray((np.sin(ang) * sign).astype(np.float32))


def kernel(x, norm_mix, w_in, b_gate, diff_lambda, diff_subln, na_rpb, qk_norm, w_branch, w_out, norm_ffn,
           w_ff1, w_ff2, norm_final):
    B, S, D = x.shape
    depth = w_in.shape[0]
    rows = B * S

    ones_col = jnp.ones((1, 3 * BRANCH_W), F32)
    qscale_col = ones_col.at[:, 0:BRANCH_W].set(QK_SCALE * LOG2E)
    a_slopes = jnp.asarray(_alibi_slopes(A_HEADS))
    cos_t, sg_t = _rope_tables(S)
    b_tabs = [_dilated_tables(S // dil, dil) for _, dil in B_PATTERNS]
    d_perm = np.array([(g * 4 + c) * HEAD_DIM + d for c in range(4) for g in range(D_KV_HEADS)
                       for d in range(HEAD_DIM)])

    o_a, o_b, o_c, o_d, o_gate = 0, 1536, 6144, 7680, 8448
    for l in range(depth):
        w = w_in[l].astype(BF16)
        nw = norm_mix[l].reshape(1, D)
        lam_init = 0.8 - 0.6 * math.exp(-0.3 * l)

        aq, ak, av = _project(x, nw, w[:, o_a:o_a + 1536], qscale_col, 1, v_ones=True)
        y_a = _flash_a(a_slopes, diff_lambda[l], diff_subln[l].reshape(PAIR_W, 1),
                       aq[:, 0], ak[:, 0], av[:, 0], lam_init, 512, 1024)

        b_outs = []
        for g, (_, dil) in enumerate(B_PATTERNS):
            lo = o_b + g * 1536
            bq, bk, bv = _project(x, nw, w[:, lo:lo + 1536], qscale_col, dil, v_ones=True)
            L = S // dil
            tab, bias = b_tabs[g]
            o, e = _windowed(tab, bq.reshape(B * dil, BRANCH_W, L), bk.reshape(B * dil, L, BRANCH_W),
                             bv.reshape(B * dil, -1, L), bias, F32, True)
            b_outs.append((o.reshape(B, dil, L, BRANCH_W), e.reshape(B, dil, L, BRANCH_W)))
        y_b = _bmerge(b_outs, S)

        cq, ck, cv = _project(x, nw, w[:, o_c:o_c + 1536], qscale_col, 1, v_ones=True)
        c_tab, c_bias = _na_tables(S, na_rpb[l])
        (y_c,) = _windowed(c_tab, cq[:, 0], ck[:, 0], cv[:, 0], c_bias, BF16, False)

        wd = jnp.concatenate([w[:, o_d:o_d + 512][:, d_perm], w[:, o_d + 512:o_gate]], axis=1)
        dq, dk, dv = _project_d(x, nw, wd, qk_norm[l, 0].reshape(HEAD_DIM, 1), qk_norm[l, 1].reshape(HEAD_DIM, 1),
                                cos_t, sg_t)
        y_d = _flash_d(dq, dk, dv, 128, 1024)

        wb = w_branch[l].astype(BF16)
        wb = wb.at[3].set(wb[3][d_perm, :])
        ys = [y.reshape(rows, BRANCH_W) for y in (y_a, y_b, y_c, y_d)]
        x2d = _merge(x.reshape(rows, D), nw, ys, w[:, o_gate:], b_gate[l].reshape(1, -1), wb,
                     w_out[l].astype(BF16), 512)

        x2d = _ffn(x2d, norm_ffn[l].reshape(1, D), w_ff1[l].astype(BF16), w_ff2[l].astype(BF16),
                   norm_final.reshape(1, D), l == depth - 1, 512)
        x = x2d.reshape(B, S, D)
    return x
```

```python
import functools
import math

import numpy as np
import jax
import jax.numpy as jnp
from jax import lax
from jax.experimental import pallas as pl
from jax.experimental.pallas import tpu as pltpu

F32 = jnp.float32
BF16 = jnp.bfloat16

D_MODEL = 1024
GRID_W = 64
HEAD_DIM = 64
PAIR_W = 2 * HEAD_DIM
N_BRANCH = 4
BRANCH_W = 512
A_HEADS = 4
B_PATTERNS = ((128, 1), (512, 4), (2048, 16))
B_HEADS = 8
B_RADIUS = 64
C_HEADS = 8
NA_KH = 8
NA_KW = 16
D_HEADS = 8
D_KV_HEADS = 2
ROPE_THETA = 10000.0
D_FF = 4 * D_MODEL
EPS = 1e-6
NEG_INF = -1e30
LOG2E = math.log2(math.e)
QK_SCALE = HEAD_DIM ** -0.5

VMEM_LIMIT_BYTES = 56 * 1024 * 1024


def _cparams(*sem):
    return pltpu.CompilerParams(dimension_semantics=sem, vmem_limit_bytes=VMEM_LIMIT_BYTES)


def _const_spec(shape):
    nd = len(shape)
    return pl.BlockSpec(shape, lambda *_: (0,) * nd)


def _rms(x, g):
    return x * lax.rsqrt(jnp.mean(x * x, axis=-1, keepdims=True) + EPS) * g


LANES = 128


def _normed_rows(x_ref, nw_ref, hs_ref, xs_ref, dil):
    rows, width = x_ref.shape
    sub = rows // dil
    h = _rms(x_ref[...], nw_ref[...])
    if dil == 1:
        hs_ref[...] = h.astype(BF16)
        return
    for c in range(width // LANES):
        xs_ref[c] = h[:, c * LANES:(c + 1) * LANES]
    for j in range(dil):
        for c in range(width // LANES):
            hs_ref[j * sub:(j + 1) * sub, c * LANES:(c + 1) * LANES] = (
                xs_ref[c, pl.ds(j, sub, stride=dil), :].astype(BF16))


ONES_ROWS = 16
VT_ROWS = PAIR_W + ONES_ROWS


def _proj_kernel(x_ref, nw_ref, w_ref, cs_ref, q_ref, k_ref, v_ref, hs_ref, xs_ref, *, dil, v_ones):
    _normed_rows(x_ref, nw_ref, hs_ref, xs_ref, dil)
    sub = x_ref.shape[0] // dil
    h = hs_ref[...]
    for idx, (o_ref, transposed) in enumerate(((q_ref, True), (k_ref, False), (v_ref, True))):
        lo = idx * BRANCH_W
        acc = jnp.dot(h, w_ref[:, lo:lo + BRANCH_W], preferred_element_type=F32)
        acc = acc * cs_ref[:, lo:lo + BRANCH_W]
        for j in range(dil):
            part = acc[j * sub:(j + 1) * sub, :]
            if o_ref is v_ref and v_ones:
                pt = part.T.astype(BF16)
                for p in range(BRANCH_W // PAIR_W):
                    o_ref[j, p * VT_ROWS:p * VT_ROWS + PAIR_W, :] = pt[p * PAIR_W:(p + 1) * PAIR_W, :]
                    o_ref[j, p * VT_ROWS + PAIR_W:(p + 1) * VT_ROWS, :] = jnp.ones((ONES_ROWS, sub), BF16)
            else:
                o_ref[j] = (part.T if transposed else part).astype(o_ref.dtype)


def _project(x, nw, w, colscale, dil, v_ones=False):
    B, S, D = x.shape
    L = S // dil
    R = max(1024, 128 * dil)
    R = min(R, S)
    sub = R // dil
    N = w.shape[1]
    vrows = (BRANCH_W // PAIR_W) * VT_ROWS if v_ones else BRANCH_W
    t_shape = jax.ShapeDtypeStruct((B, dil, BRANCH_W, L), BF16)
    n_shape = jax.ShapeDtypeStruct((B, dil, L, BRANCH_W), BF16)
    v_shape = jax.ShapeDtypeStruct((B, dil, vrows, L), BF16)
    t_spec = pl.BlockSpec((None, dil, BRANCH_W, sub), lambda b, i: (b, 0, 0, i))
    n_spec = pl.BlockSpec((None, dil, sub, BRANCH_W), lambda b, i: (b, 0, i, 0))
    v_spec = pl.BlockSpec((None, dil, vrows, sub), lambda b, i: (b, 0, 0, i))
    return pl.pallas_call(
        functools.partial(_proj_kernel, dil=dil, v_ones=v_ones),
        grid=(B, S // R),
        in_specs=[pl.BlockSpec((None, R, D), lambda b, i: (b, i, 0)),
                  _const_spec((1, D)), _const_spec((D, N)), _const_spec((1, N))],
        out_specs=[t_spec, n_spec, v_spec],
        out_shape=[t_shape, n_shape, v_shape],
        scratch_shapes=[pltpu.VMEM((R, D), BF16),
                        pltpu.VMEM((D // LANES, R if dil > 1 else 8, LANES), F32)],
        compiler_params=_cparams("parallel", "parallel"),
        name=f"proj_dil{dil}",
    )(x, nw, w, colscale)


def _norm_rope_t(yt, g, cos, sg):
    outs = []
    for h in range(yt.shape[0] // HEAD_DIM):
        yh = yt[h * HEAD_DIM:(h + 1) * HEAD_DIM, :]
        yn = yh * lax.rsqrt(jnp.mean(yh * yh, axis=0, keepdims=True) + EPS) * g
        swapped = jnp.concatenate([yn[16:32], yn[0:16], yn[48:64], yn[32:48]], axis=0)
        outs.append(yn * cos + swapped * sg)
    return jnp.concatenate(outs, axis=0)


def _proj_d_kernel(x_ref, nw_ref, w_ref, gq_ref, gk_ref, cos_ref, sg_ref, q_ref, k_ref, v_ref, hs_ref):
    _normed_rows(x_ref, nw_ref, hs_ref, None, 1)
    h = hs_ref[...]
    cos = cos_ref[...]
    sg = sg_ref[...]
    qw = D_HEADS * HEAD_DIM
    kw = D_KV_HEADS * HEAD_DIM
    qt = jnp.dot(h, w_ref[:, 0:qw], preferred_element_type=F32).T
    q_ref[...] = (_norm_rope_t(qt, gq_ref[...], cos, sg) * (QK_SCALE * LOG2E)).astype(BF16)
    kt = jnp.dot(h, w_ref[:, qw:qw + kw], preferred_element_type=F32).T
    k_ref[...] = _norm_rope_t(kt, gk_ref[...], cos, sg).T.astype(BF16)
    v_ref[0:kw, :] = jnp.dot(h, w_ref[:, qw + kw:qw + 2 * kw], preferred_element_type=F32).T.astype(BF16)
    v_ref[kw:, :] = jnp.ones((ONES_ROWS, v_ref.shape[1]), BF16)


def _project_d(x, nw, w, gq, gk, cos_t, sg_t):
    B, S, D = x.shape
    R = min(1024, S)
    qw = D_HEADS * HEAD_DIM
    kw = D_KV_HEADS * HEAD_DIM
    return pl.pallas_call(
        _proj_d_kernel,
        grid=(B, S // R),
        in_specs=[pl.BlockSpec((None, R, D), lambda b, i: (b, i, 0)),
                  _const_spec((1, D)), _const_spec((D, qw + 2 * kw)),
                  _const_spec((HEAD_DIM, 1)), _const_spec((HEAD_DIM, 1)),
                  pl.BlockSpec((HEAD_DIM, R), lambda b, i: (0, i)),
                  pl.BlockSpec((HEAD_DIM, R), lambda b, i: (0, i))],
        out_specs=[pl.BlockSpec((None, qw, R), lambda b, i: (b, 0, i)),
                   pl.BlockSpec((None, R, kw), lambda b, i: (b, i, 0)),
                   pl.BlockSpec((None, VT_ROWS, R), lambda b, i: (b, 0, i))],
        out_shape=[jax.ShapeDtypeStruct((B, qw, S), BF16),
                   jax.ShapeDtypeStruct((B, S, kw), BF16),
                   jax.ShapeDtypeStruct((B, VT_ROWS, S), BF16)],
        scratch_shapes=[pltpu.VMEM((R, D), BF16)],
        compiler_params=_cparams("parallel", "parallel"),
        name="proj_gqa",
    )(x, nw, w, gq, gk, cos_t, sg_t)


FLASH_COLS = 256


def _group_spans(c0, width):
    spans, c = [], c0
    while c < c0 + width:
        g, lo = divmod(c, FLASH_COLS)
        n = min(FLASH_COLS - lo, c0 + width - c)
        spans.append((g, lo, n))
        c += n
    return spans


def _read_cols(ref, rows, c0, width):
    parts = [ref[g, rows, lo:lo + n] for g, lo, n in _group_spans(c0, width)]
    return parts[0] if len(parts) == 1 else jnp.concatenate(parts, axis=1)


def _stack_pair_queries(q2_ref, qt_ref, nchunk, tq):
    row = lax.broadcasted_iota(jnp.int32, (PAIR_W, tq), 0)
    for c in range(nchunk):
        qc = qt_ref[c * PAIR_W:(c + 1) * PAIR_W, :].astype(F32)
        for half, keep in enumerate((row < HEAD_DIM, row >= HEAD_DIM)):
            val = jnp.where(keep, qc, 0.0).astype(BF16)
            off = 0
            for g, lo, n in _group_spans((2 * c + half) * tq, tq):
                q2_ref[g, :, lo:lo + n] = val[:, off:off + n]
                off += n


def _flash_scratch(M, tk):
    G = M // FLASH_COLS
    stat = pltpu.VMEM((G, 1, FLASH_COLS), F32)
    probs = pltpu.VMEM((G, tk, FLASH_COLS), BF16)
    return [pltpu.VMEM((G, PAIR_W, FLASH_COLS), BF16), stat, stat,
            pltpu.VMEM((G, VT_ROWS, FLASH_COLS), F32), probs,
            pltpu.VMEM((G, tk, FLASH_COLS), F32), pltpu.VMEM((G, tk, FLASH_COLS), F32), stat, stat,
            probs, stat, stat]


def _flash_loop(k_ref, vt_ref, q2_ref, m_ref, a_ref, acc_ref, p_ref, sa_ref, sb_ref, ma_ref, mb_ref,
                *, tq, tk, seq, bias_fn):
    ngroups = q2_ref.shape[0]
    nk = seq // tk
    m_ref[...] = jnp.full(m_ref.shape, NEG_INF, F32)
    acc_ref[...] = jnp.zeros(acc_ref.shape, F32)

    def scores(j, g, s_ref, mx_ref):
        k = k_ref[pl.ds(pl.multiple_of(j * tk, tk), tk), :]
        s = jnp.dot(k, q2_ref[g], preferred_element_type=F32)
        if bias_fn is not None:
            tile, shift, _ = bias_fn(j, (g * FLASH_COLS) % tq)
            s = s + tile + shift
        s_ref[g] = s
        mx_ref[g] = jnp.max(s, axis=0, keepdims=True)

    def probs(g, s_ref, mx_ref):
        m_old = m_ref[g]
        m_new = jnp.maximum(m_old, mx_ref[g])
        m_ref[g] = m_new
        a_ref[g] = jnp.exp2(m_old - m_new)
        p_ref[g] = jnp.exp2((s_ref[g] - m_new).astype(BF16))

    def values(j, g):
        vt = vt_ref[:, pl.ds(pl.multiple_of(j * tk, tk), tk)]
        acc_ref[g] = acc_ref[g] * a_ref[g] + jnp.dot(vt, p_ref[g], preferred_element_type=F32)

    def step(j, cur, nxt):
        for g in range(ngroups):
            if nxt is not None:
                scores(j + 1, g, *nxt)
            probs(g, *cur)
            if g > 0:
                values(j, g - 1)
        values(j, ngroups - 1)

    buf_a, buf_b = (sa_ref, ma_ref), (sb_ref, mb_ref)
    for g in range(ngroups):
        scores(0, g, *buf_a)

    def body(i, carry):
        step(2 * i, buf_a, buf_b)
        step(2 * i + 1, buf_b, buf_a)
        return carry

    lax.fori_loop(0, nk // 2 - 1, body, 0)
    step(nk - 2, buf_a, buf_b)
    step(nk - 1, buf_b, None)


LAG_LIMIT = 64.0
FLASH_PROBE_ROWS = 128


def _flash_attend(k_ref, vt_ref, q2_ref, m_ref, r_ref, acc_ref, pa_ref, sa_ref, sb_ref, ra_ref, rb_ref,
                  pb_ref, lag_ref, mu_ref, *, tq, tk, seq, bias_fn):
    ngroups = q2_ref.shape[0]
    nk = seq // tk

    def block_scores(j, g, rows=tk):
        k = k_ref[pl.ds(pl.multiple_of(j * tk, tk), rows), :]
        s = jnp.dot(k, q2_ref[g], preferred_element_type=F32)
        if bias_fn is None:
            return s, 0.0, 0.0
        tile, shift, bias_max = bias_fn(j, (g * FLASH_COLS) % tq, rows)
        return s + tile, shift, bias_max

    for g in range(ngroups):
        s, shift, bias_max = block_scores(0, g, FLASH_PROBE_ROWS)
        first = jnp.max(s, axis=0, keepdims=True) + shift
        m_ref[g] = first
        mu_ref[g] = first - bias_max
        r_ref[g] = first
    acc_ref[...] = jnp.zeros(acc_ref.shape, F32)
    lag_ref[...] = jnp.zeros(lag_ref.shape, F32)

    def scores(j, g, p_ref, rp_ref):
        s, shift, bias_max = block_scores(j, g)
        m_old = m_ref[g]
        ref = jnp.maximum(m_old, mu_ref[g] + bias_max)
        rp_ref[g] = ref
        p_ref[g] = jnp.exp2((s - (ref - shift)).astype(BF16))
        mx = jnp.max(s, axis=0, keepdims=True) + shift
        m_new = jnp.maximum(m_old, mx)
        lag_ref[g] = jnp.maximum(lag_ref[g], jnp.maximum(mx - ref, ref - m_new))
        m_ref[g] = m_new
        mu_ref[g] = jnp.maximum(mu_ref[g], mx - bias_max)

    def values(j, g, p_ref, rp_ref):
        vt = vt_ref[:, pl.ds(pl.multiple_of(j * tk, tk), tk)]
        ref = rp_ref[g]
        acc_ref[g] = (acc_ref[g] * jnp.exp2(r_ref[g] - ref)
                      + jnp.dot(vt, p_ref[g], preferred_element_type=F32))
        r_ref[g] = ref

    def step(j, cur, nxt):
        for g in range(ngroups):
            if nxt is not None:
                scores(j + 1, g, *nxt)
            values(j, g, *cur)

    buf_a, buf_b = (pa_ref, ra_ref), (pb_ref, rb_ref)
    for g in range(ngroups):
        scores(0, g, *buf_a)

    def body(i, carry):
        step(2 * i, buf_a, buf_b)
        step(2 * i + 1, buf_b, buf_a)
        return carry

    lax.fori_loop(0, nk // 2 - 1, body, 0)
    step(nk - 2, buf_a, buf_b)
    step(nk - 1, buf_b, None)

    @pl.when(jnp.max(lag_ref[...]) > LAG_LIMIT)
    def _():
        _flash_loop(k_ref, vt_ref, q2_ref, m_ref, r_ref, acc_ref, pa_ref, sa_ref, sb_ref, ra_ref, rb_ref,
                    tq=tq, tk=tk, seq=seq, bias_fn=bias_fn)


def _flash_a_kernel(slope_ref, lam_ref, g_ref, qt_ref, k_ref, vt_ref, o_ref,
                    q2_ref, m_ref, a_ref, acc_ref, *bufs, tq, tk, seq, lam_init):
    _stack_pair_queries(q2_ref, qt_ref, 1, tq)
    *bufs, tiles_ref = bufs
    negc2 = -LOG2E * slope_ref[pl.program_id(1)]
    q0 = pl.program_id(2) * tq
    @pl.when(pl.program_id(2) == 0)
    def _():
        d0 = (lax.broadcasted_iota(jnp.int32, (tk, FLASH_COLS), 1)
              - lax.broadcasted_iota(jnp.int32, (tk, FLASH_COLS), 0)).astype(F32)
        tiles_ref[0] = d0 * negc2
        tiles_ref[1] = d0 * (-negc2)
        for n in range(tk // FLASH_COLS):
            tiles_ref[2 + n] = jnp.abs(d0 + float(n * FLASH_COLS)) * negc2

    qcol = lax.broadcasted_iota(jnp.int32, (1, FLASH_COLS), 1).astype(F32)

    def bias_fn(j, qcol0, rows=tk):
        delta = q0 + qcol0 - j * tk
        right, left = delta >= tk, delta <= -FLASH_COLS
        idx = jnp.where(right, 0, jnp.where(left, 1, 2 + jnp.clip(delta // FLASH_COLS, 0, tk // FLASH_COLS - 1)))
        shift = jnp.where(right | left, negc2 * jnp.abs(delta).astype(F32), 0.0)
        t = qcol + delta.astype(F32)
        nearest = jnp.maximum(jnp.maximum(t - float(rows - 1), -t), 0.0)
        return tiles_ref[idx, 0:rows, :], shift, negc2 * nearest

    _flash_attend(k_ref, vt_ref, q2_ref, m_ref, a_ref, acc_ref, *bufs,
                tq=tq, tk=tk, seq=seq, bias_fn=bias_fn)

    lp = lam_ref[...]
    lam = (jnp.exp(jnp.sum(lp[0:1] * lp[1:2], axis=1, keepdims=True))
           - jnp.exp(jnp.sum(lp[2:3] * lp[3:4], axis=1, keepdims=True)) + lam_init)
    vals, sums = slice(0, PAIR_W), slice(PAIR_W, PAIR_W + 1)
    o = (_read_cols(acc_ref, vals, 0, tq) / _read_cols(acc_ref, sums, 0, tq)
         - lam * (_read_cols(acc_ref, vals, tq, tq) / _read_cols(acc_ref, sums, tq, tq)))
    o = o * lax.rsqrt(jnp.mean(o * o, axis=0, keepdims=True) + EPS) * g_ref[...]
    o_ref[...] = (o * (1.0 - lam_init)).T.astype(o_ref.dtype)


def _flash_a(slopes, lamp, subg, qt, k, vt, lam_init, tq, tk):
    B, _, S = qt.shape
    tq, tk = min(tq, S), min(tk, S)
    M = 2 * tq
    return pl.pallas_call(
        functools.partial(_flash_a_kernel, tq=tq, tk=tk, seq=S, lam_init=lam_init),
        grid=(B, A_HEADS, S // tq),
        in_specs=[pl.BlockSpec(memory_space=pltpu.SMEM),
                  _const_spec((4, HEAD_DIM)), _const_spec((PAIR_W, 1)),
                  pl.BlockSpec((None, PAIR_W, tq), lambda b, h, i: (b, h, i)),
                  pl.BlockSpec((None, S, PAIR_W), lambda b, h, i: (b, 0, h)),
                  pl.BlockSpec((None, VT_ROWS, S), lambda b, h, i: (b, h, 0))],
        out_specs=pl.BlockSpec((None, tq, PAIR_W), lambda b, h, i: (b, i, h)),
        out_shape=jax.ShapeDtypeStruct((B, S, A_HEADS * PAIR_W), BF16),
        scratch_shapes=_flash_scratch(M, tk) + [pltpu.VMEM((2 + tk // FLASH_COLS, tk, FLASH_COLS), F32)],
        compiler_params=_cparams("parallel", "parallel", "arbitrary"),
        name="flash_diff",
    )(slopes, lamp, subg, qt, k, vt)


def _flash_d_kernel(qt_ref, k_ref, vt_ref, o_ref, q2_ref, m_ref, a_ref, acc_ref, *bufs, tq, tk, seq):
    nchunk = qt_ref.shape[0] // PAIR_W
    _stack_pair_queries(q2_ref, qt_ref, nchunk, tq)
    _flash_attend(k_ref, vt_ref, q2_ref, m_ref, a_ref, acc_ref, *bufs,
                tq=tq, tk=tk, seq=seq, bias_fn=None)
    sums = slice(PAIR_W, PAIR_W + 1)
    for c in range(nchunk):
        top, bot = (2 * c) * tq, (2 * c + 1) * tq
        ot = jnp.concatenate(
            [_read_cols(acc_ref, slice(0, HEAD_DIM), top, tq) / _read_cols(acc_ref, sums, top, tq),
             _read_cols(acc_ref, slice(HEAD_DIM, PAIR_W), bot, tq) / _read_cols(acc_ref, sums, bot, tq)],
            axis=0)
        o_ref[:, c * PAIR_W:(c + 1) * PAIR_W] = ot.T.astype(o_ref.dtype)


def _flash_d(qt, k, vt, tq, tk):
    B, qw, S = qt.shape
    tq, tk = min(tq, S), min(tk, S)
    M = 2 * (qw // PAIR_W) * tq
    return pl.pallas_call(
        functools.partial(_flash_d_kernel, tq=tq, tk=tk, seq=S),
        grid=(B, S // tq),
        in_specs=[pl.BlockSpec((None, qw, tq), lambda b, i: (b, 0, i)),
                  pl.BlockSpec((None, S, PAIR_W), lambda b, i: (b, 0, 0)),
                  pl.BlockSpec((None, VT_ROWS, S), lambda b, i: (b, 0, 0))],
        out_specs=pl.BlockSpec((None, tq, qw), lambda b, i: (b, i, 0)),
        out_shape=jax.ShapeDtypeStruct((B, S, qw), BF16),
        scratch_shapes=_flash_scratch(M, tk),
        compiler_params=_cparams("parallel", "parallel"),
        name="flash_gqa",
    )(qt, k, vt)


WIN_TQ = 128
WIN_UNROLL = 8


def _win_kernel(tab_ref, qt_ref, k_ref, vt_ref, bias_ref, o_ref, *lse_refs, width, nblocks, unroll):
    row = lax.broadcasted_iota(jnp.int32, (PAIR_W, WIN_TQ), 0)
    pairs = qt_ref.shape[0] // PAIR_W

    def scores(pi, i):
        lanes = slice(pi * PAIR_W, (pi + 1) * PAIR_W)
        ws = pl.multiple_of(tab_ref[i, 0], 128)
        q0 = pl.multiple_of(i * WIN_TQ, WIN_TQ)
        qc = qt_ref[lanes, pl.ds(q0, WIN_TQ)].astype(F32)
        q2 = jnp.concatenate([jnp.where(row < HEAD_DIM, qc, 0.0),
                              jnp.where(row >= HEAD_DIM, qc, 0.0)], axis=1).astype(BF16)
        s = jnp.dot(k_ref[pl.ds(ws, width), lanes], q2, preferred_element_type=F32)
        return pi, lanes, ws, q0, s + bias_ref[tab_ref[i, 1], pi]

    def finish(lanes, q0, m, acc):
        l = acc[PAIR_W:PAIR_W + 1, :]
        rl = 1.0 / l
        ot = jnp.concatenate([acc[0:HEAD_DIM, 0:WIN_TQ] * rl[:, 0:WIN_TQ],
                              acc[HEAD_DIM:PAIR_W, WIN_TQ:] * rl[:, WIN_TQ:]], axis=0)
        o_ref[pl.ds(q0, WIN_TQ), lanes] = ot.T.astype(o_ref.dtype)
        if lse_refs:
            lse = m + jnp.log2(l)
            lt = jnp.concatenate([jnp.broadcast_to(lse[:, 0:WIN_TQ], (HEAD_DIM, WIN_TQ)),
                                  jnp.broadcast_to(lse[:, WIN_TQ:], (HEAD_DIM, WIN_TQ))], axis=0)
            lse_refs[0][pl.ds(q0, WIN_TQ), lanes] = lt.T

    def body(ib, carry):
        blocks = [scores(pi, ib * unroll + u) for pi in range(pairs) for u in range(unroll)]
        maxes, probs, accs = [], [], []

        def values(n):
            pi, ws = blocks[n][0], blocks[n][2]
            vt = vt_ref[pi * VT_ROWS:(pi + 1) * VT_ROWS, pl.ds(ws, width)]
            accs.append(jnp.dot(vt, probs[n], preferred_element_type=F32))

        for n, (*_, s) in enumerate(blocks):
            m = jnp.max(s, axis=0, keepdims=True)
            maxes.append(m)
            probs.append(jnp.exp2((s - m).astype(BF16)))
            if n > 0:
                values(n - 1)
        values(len(blocks) - 1)
        for (_, lanes, _, q0, _), m, acc in zip(blocks, maxes, accs):
            finish(lanes, q0, m, acc)
        return carry

    lax.fori_loop(0, nblocks // unroll, body, 0)


def _windowed(tab, qt, k, vt, bias, out_dtype, want_lse):
    N, _, L = qt.shape
    nvar, npair, width, _ = bias.shape
    nblocks = L // WIN_TQ
    unroll = min(WIN_UNROLL, nblocks)
    pp = min(npair, WIN_UNROLL // unroll)
    o_spec = pl.BlockSpec((None, L, pp * PAIR_W), lambda n, p: (n, 0, p))
    out_shape = [jax.ShapeDtypeStruct((N, L, npair * PAIR_W), out_dtype)]
    out_specs = [o_spec]
    if want_lse:
        out_shape.append(jax.ShapeDtypeStruct((N, L, npair * PAIR_W), F32))
        out_specs.append(o_spec)
    return pl.pallas_call(
        functools.partial(_win_kernel, width=width, nblocks=nblocks, unroll=unroll),
        grid=(N, npair // pp),
        in_specs=[pl.BlockSpec(memory_space=pltpu.SMEM),
                  pl.BlockSpec((None, pp * PAIR_W, L), lambda n, p: (n, p, 0)),
                  pl.BlockSpec((None, L, pp * PAIR_W), lambda n, p: (n, 0, p)),
                  pl.BlockSpec((None, pp * VT_ROWS, L), lambda n, p: (n, p, 0)),
                  pl.BlockSpec((nvar, pp, width, 2 * WIN_TQ), lambda n, p: (0, p, 0, 0))],
        out_specs=out_specs,
        out_shape=out_shape,
        compiler_params=_cparams("parallel", "parallel"),
        name=f"windowed_w{width}",
    )(tab, qt, k, vt, bias)


def _bmerge_kernel(o0_ref, l0_ref, o1_ref, l1_ref, o2_ref, l2_ref, y_ref, so1, sl1, so2, sl2, *, dils):
    T, width = y_ref.shape
    nchunk = width // LANES
    for (o_ref, l_ref, so, sl, dil) in ((o1_ref, l1_ref, so1, sl1, dils[1]), (o2_ref, l2_ref, so2, sl2, dils[2])):
        sub = T // dil
        for j in range(dil):
            for c in range(nchunk):
                so[c, pl.ds(j, sub, stride=dil), :] = o_ref[j, :, c * LANES:(c + 1) * LANES]
                sl[c, pl.ds(j, sub, stride=dil), :] = l_ref[j, :, c * LANES:(c + 1) * LANES]
    for c in range(nchunk):
        cols = slice(c * LANES, (c + 1) * LANES)
        o0, e0 = o0_ref[0, :, cols], l0_ref[0, :, cols]
        e1, e2 = sl1[c], sl2[c]
        mx = jnp.maximum(jnp.maximum(e0, e1), e2)
        w0, w1, w2 = jnp.exp2(e0 - mx), jnp.exp2(e1 - mx), jnp.exp2(e2 - mx)
        num = w0 * o0 + w1 * so1[c] + w2 * so2[c]
        y_ref[:, cols] = (num / (w0 + w1 + w2)).astype(y_ref.dtype)


def _bmerge(outs, S):
    B = outs[0][0].shape[0]
    dils = tuple(d for _, d in B_PATTERNS)
    T = min(512, S)
    args, in_specs = [], []
    for (o, e), dil in zip(outs, dils):
        spec = pl.BlockSpec((None, dil, T // dil, BRANCH_W), lambda b, t: (b, 0, t, 0))
        args += [o, e]
        in_specs += [spec, spec]
    return pl.pallas_call(
        functools.partial(_bmerge_kernel, dils=dils),
        grid=(B, S // T),
        in_specs=in_specs,
        out_specs=pl.BlockSpec((None, T, BRANCH_W), lambda b, t: (b, t, 0)),
        out_shape=jax.ShapeDtypeStruct((B, S, BRANCH_W), BF16),
        scratch_shapes=[pltpu.VMEM((BRANCH_W // LANES, T, LANES), F32)] * 4,
        compiler_params=_cparams("parallel", "parallel"),
        name="dilated_merge",
    )(*args)


def _merge_kernel(x_ref, nw_ref, ya_ref, yb_ref, yc_ref, yd_ref, wg_ref, bg_ref, wb_ref, wo_ref, o_ref):
    x = x_ref[...]
    h = _rms(x, nw_ref[...]).astype(BF16)
    merged = None
    for n, y_ref in enumerate((ya_ref, yb_ref, yc_ref, yd_ref)):
        cols = slice(n * D_MODEL, (n + 1) * D_MODEL)
        z = jnp.dot(h, wg_ref[:, cols], preferred_element_type=F32) + bg_ref[:, cols]
        gate = 1.0 / (1.0 + jnp.exp(-z))
        t = gate * jnp.dot(y_ref[...], wb_ref[n], preferred_element_type=F32)
        merged = t if merged is None else merged + t
    o_ref[...] = x + jnp.dot(merged.astype(BF16), wo_ref[...], preferred_element_type=F32)


def _merge(x2d, nw, ys, wg, bg, wb, wo, tm):
    Mrows, D = x2d.shape
    tm = min(tm, Mrows)
    row = lambda w: pl.BlockSpec((tm, w), lambda i: (i, 0))
    return pl.pallas_call(
        _merge_kernel,
        grid=(Mrows // tm,),
        in_specs=[row(D), _const_spec((1, D)), row(BRANCH_W), row(BRANCH_W), row(BRANCH_W), row(BRANCH_W),
                  _const_spec(wg.shape), _const_spec(bg.shape), _const_spec(wb.shape), _const_spec(wo.shape)],
        out_specs=row(D),
        out_shape=jax.ShapeDtypeStruct((Mrows, D), F32),
        compiler_params=_cparams("parallel"),
        name="gated_merge",
    )(x2d, nw, *ys, wg, bg, wb, wo)


FFN_CHUNK = 1024


def _ffn_kernel(x_ref, nw_ref, w1_ref, w2_ref, nf_ref, o_ref, *, final):
    x = x_ref[...]
    h = _rms(x, nw_ref[...]).astype(BF16)
    acc = x
    for c0 in range(0, D_FF, FFN_CHUNK):
        u = jnp.maximum(jnp.dot(h, w1_ref[:, c0:c0 + FFN_CHUNK], preferred_element_type=F32), 0.0)
        acc = acc + jnp.dot((u * u).astype(BF16), w2_ref[c0:c0 + FFN_CHUNK, :], preferred_element_type=F32)
    if final:
        acc = _rms(acc, nf_ref[...])
    o_ref[...] = acc


def _ffn(x2d, nw, w1, w2, nf, final, tm):
    Mrows, D = x2d.shape
    tm = min(tm, Mrows)
    row = pl.BlockSpec((tm, D), lambda i: (i, 0))
    return pl.pallas_call(
        functools.partial(_ffn_kernel, final=final),
        grid=(Mrows // tm,),
        in_specs=[row, _const_spec((1, D)), _const_spec(w1.shape), _const_spec(w2.shape), _const_spec((1, D))],
        out_specs=row,
        out_shape=jax.ShapeDtypeStruct((Mrows, D), F32),
        compiler_params=_cparams("parallel"),
        name="relu2_mlp",
    )(x2d, nw, w1, w2, nf)


def _alibi_slopes(n):
    return np.array([2.0 ** (-8.0 * (i + 1) / n) for i in range(n)], dtype=np.float32)


def _pair_tiles(t):
    nvar, nh, w, tq = t.shape
    t = t.reshape(nvar, nh // 2, 2, w, tq)
    return jnp.concatenate([t[:, :, 0], t[:, :, 1]], axis=-1)


def _dilated_tables(L, dil):
    width = 3 * WIN_TQ
    nb = L // WIN_TQ
    ws = np.clip((np.arange(nb) - 1) * WIN_TQ, 0, L - width)
    var = np.where(np.arange(nb) == 0, 0, np.where(np.arange(nb) == nb - 1, 2, 1))
    tab = np.stack([ws, var], axis=1).astype(np.int32)
    c = np.arange(width)[:, None]
    r = np.arange(WIN_TQ)[None, :]
    slopes = _alibi_slopes(B_HEADS)
    tiles = []
    for shift in (0, WIN_TQ, 2 * WIN_TQ):
        rel = c - shift - r
        valid = np.abs(rel) <= B_RADIUS
        bias = -slopes[:, None, None] * (np.abs(rel) * dil).astype(np.float32)[None] * np.float32(LOG2E)
        tiles.append(np.where(valid[None], bias, np.float32(NEG_INF)))
    return jnp.asarray(tab), _pair_tiles(jnp.asarray(np.stack(tiles).astype(np.float32)))


def _na_tables(S, rpb):
    rows = S // GRID_W
    kh = min(NA_KH, rows)
    wrows = 10
    nb = rows // 2
    r0s = 2 * np.arange(nb)
    wsr = np.clip(r0s - kh // 2, 0, rows - wrows)
    tab = np.stack([wsr * GRID_W, (r0s - wsr) // 2], axis=1).astype(np.int32)
    reps = [0, 2, 4, rows - 4, rows - 2]
    krl = np.arange(wrows)
    kc = np.arange(GRID_W)
    qrl = np.arange(2)
    qc = np.arange(GRID_W)
    qstart = np.clip(qc - NA_KW // 2, 0, GRID_W - NA_KW)
    col_ok = (kc[:, None] >= qstart[None, :]) & (kc[:, None] < qstart[None, :] + NA_KW)
    ridx, valid = [], []
    for r0 in reps:
        w0 = int(np.clip(r0 - kh // 2, 0, rows - wrows))
        r = (r0 + qrl)[None, :]
        rs = np.clip(r - kh // 2, 0, rows - kh)
        kr = (w0 + krl)[:, None]
        row_ok = (kr >= rs) & (kr < rs + kh)
        ridx.append(np.clip(kr - r + NA_KH - 1, 0, 2 * NA_KH - 2))
        valid.append(row_ok[:, None, :, None] & col_ok[None, :, None, :])
    ridx, valid = np.stack(ridx), np.stack(valid)
    nh = rpb.shape[0]
    span = GRID_W - NA_KW
    padded = jnp.pad(rpb.astype(F32), ((0, 0), (0, 0), (span, span)), mode="edge")
    period = 2 * GRID_W
    padded = jnp.pad(padded, ((0, 0), (0, 0), (0, 1)))
    shifted = jnp.tile(padded, (1, 1, GRID_W))[:, :, :GRID_W * (period - 1)]
    shifted = shifted.reshape(nh, -1, GRID_W, period - 1)
    by_col = jnp.swapaxes(shifted[:, :, :, GRID_W - 1:period - 1], -1, -2)
    slabs = jnp.take(by_col, jnp.asarray(ridx.reshape(-1)), axis=1)
    slabs = slabs.reshape(nh, len(reps), wrows, 2, GRID_W, GRID_W).transpose(1, 0, 2, 4, 3, 5)
    tiles = jnp.where(valid[:, None], slabs * LOG2E, NEG_INF)
    tiles = tiles.reshape(len(reps), nh, wrows * GRID_W, 2 * GRID_W)
    return jnp.asarray(tab), _pair_tiles(tiles)


def _rope_tables(S):
    n = HEAD_DIM // 4
    pos = np.arange(S)
    inv = ROPE_THETA ** (-np.arange(n, dtype=np.float32) / n)
    d = np.arange(HEAD_DIM)
    p = np.where((d < HEAD_DIM // 2)[:, None], (pos // GRID_W)[None, :], (pos % GRID_W)[None, :]).astype(np.float32)
    ang = p * inv[d % n][:, None]
    sign = np.where((d % (2 * n)) < n, -1.0, 1.0).astype(np.float32)[:, None]
    return jnp.asarray(np.cos(ang).astype(np.float32)), jnp.asarray((np.sin(ang) * sign).astype(np.float32))


def kernel(x, norm_mix, w_in, b_gate, diff_lambda, diff_subln, na_rpb, qk_norm, w_branch, w_out, norm_ffn,
           w_ff1, w_ff2, norm_final):
    B, S, D = x.shape
    depth = w_in.shape[0]
    rows = B * S

    ones_col = jnp.ones((1, 3 * BRANCH_W), F32)
    qscale_col = ones_col.at[:, 0:BRANCH_W].set(QK_SCALE * LOG2E)
    a_slopes = jnp.asarray(_alibi_slopes(A_HEADS))
    cos_t, sg_t = _rope_tables(S)
    b_tabs = [_dilated_tables(S // dil, dil) for _, dil in B_PATTERNS]
    d_perm = np.array([(g * 4 + c) * HEAD_DIM + d for c in range(4) for g in range(D_KV_HEADS)
                       for d in range(HEAD_DIM)])

    o_a, o_b, o_c, o_d, o_gate = 0, 1536, 6144, 7680, 8448
    for l in range(depth):
        w = w_in[l].astype(BF16)
        nw = norm_mix[l].reshape(1, D)
        lam_init = 0.8 - 0.6 * math.exp(-0.3 * l)

        aq, ak, av = _project(x, nw, w[:, o_a:o_a + 1536], qscale_col, 1, v_ones=True)
        y_a = _flash_a(a_slopes, diff_lambda[l], diff_subln[l].reshape(PAIR_W, 1),
                       aq[:, 0], ak[:, 0], av[:, 0], lam_init, 1024, 1024)

        b_outs = []
        for g, (_, dil) in enumerate(B_PATTERNS):
            lo = o_b + g * 1536
            bq, bk, bv = _project(x, nw, w[:, lo:lo + 1536], qscale_col, dil, v_ones=True)
            L = S // dil
            tab, bias = b_tabs[g]
            o, e = _windowed(tab, bq.reshape(B * dil, BRANCH_W, L), bk.reshape(B * dil, L, BRANCH_W),
                             bv.reshape(B * dil, -1, L), bias, F32, True)
            b_outs.append((o.reshape(B, dil, L, BRANCH_W), e.reshape(B, dil, L, BRANCH_W)))
        y_b = _bmerge(b_outs, S)

        cq, ck, cv = _project(x, nw, w[:, o_c:o_c + 1536], qscale_col, 1, v_ones=True)
        c_tab, c_bias = _na_tables(S, na_rpb[l])
        (y_c,) = _windowed(c_tab, cq[:, 0], ck[:, 0], cv[:, 0], c_bias, BF16, False)

        wd = jnp.concatenate([w[:, o_d:o_d + 512][:, d_perm], w[:, o_d + 512:o_gate]], axis=1)
        dq, dk, dv = _project_d(x, nw, wd, qk_norm[l, 0].reshape(HEAD_DIM, 1), qk_norm[l, 1].reshape(HEAD_DIM, 1),
                                cos_t, sg_t)
        y_d = _flash_d(dq, dk, dv, 256, 1024)

        wb = w_branch[l].astype(BF16)
        wb = wb.at[3].set(wb[3][d_perm, :])
        ys = [y.reshape(rows, BRANCH_W) for y in (y_a, y_b, y_c, y_d)]
        x2d = _merge(x.reshape(rows, D), nw, ys, w[:, o_gate:], b_gate[l].reshape(1, -1), wb,
                     w_out[l].astype(BF16), 512)

        x2d = _ffn(x2d, norm_ffn[l].reshape(1, D), w_ff1[l].astype(BF16), w_ff2[l].astype(BF16),
                   norm_final.reshape(1, D), l == depth - 1, 512)
        x = x2d.reshape(B, S, D)
    return x
```

```python
import functools
import math

import numpy as np
import jax
import jax.numpy as jnp
from jax import lax
from jax.experimental import pallas as pl
from jax.experimental.pallas import tpu as pltpu

F32 = jnp.float32
BF16 = jnp.bfloat16

D_MODEL = 1024
GRID_W = 64
HEAD_DIM = 64
PAIR_W = 2 * HEAD_DIM
N_BRANCH = 4
BRANCH_W = 512
A_HEADS = 4
B_PATTERNS = ((128, 1), (512, 4), (2048, 16))
B_HEADS = 8
B_RADIUS = 64
C_HEADS = 8
NA_KH = 8
NA_KW = 16
D_HEADS = 8
D_KV_HEADS = 2
ROPE_THETA = 10000.0
D_FF = 4 * D_MODEL
EPS = 1e-6
NEG_INF = -1e30
LOG2E = math.log2(math.e)
QK_SCALE = HEAD_DIM ** -0.5

VMEM_LIMIT_BYTES = 56 * 1024 * 1024
VREG_ELEMS = 8 * 128

FLASH_TQ_DIFF = 1024
FLASH_TQ_GQA = 256
FLASH_TK = 1024
PROJ_ROWS = 1024
DENSE_ROWS = 512


def _cparams(*sem):
    return pltpu.CompilerParams(dimension_semantics=sem, vmem_limit_bytes=VMEM_LIMIT_BYTES)


def _const_spec(shape):
    nd = len(shape)
    return pl.BlockSpec(shape, lambda *_: (0,) * nd)


def _rms(x, g):
    return x * lax.rsqrt(jnp.mean(x * x, axis=-1, keepdims=True) + EPS) * g


LANES = 128


def _normed_rows(x_ref, nw_ref, hs_ref, xs_ref, dil):
    rows, width = x_ref.shape
    sub = rows // dil
    h = _rms(x_ref[...], nw_ref[...])
    if dil == 1:
        hs_ref[...] = h.astype(BF16)
        return
    for c in range(width // LANES):
        xs_ref[c] = h[:, c * LANES:(c + 1) * LANES]
    for j in range(dil):
        for c in range(width // LANES):
            hs_ref[j * sub:(j + 1) * sub, c * LANES:(c + 1) * LANES] = (
                xs_ref[c, pl.ds(j, sub, stride=dil), :].astype(BF16))


ONES_ROWS = 16
VT_ROWS = PAIR_W + ONES_ROWS


def _proj_kernel(x_ref, nw_ref, w_ref, cs_ref, q_ref, k_ref, v_ref, hs_ref, xs_ref, *, dil, v_ones):
    _normed_rows(x_ref, nw_ref, hs_ref, xs_ref, dil)
    sub = x_ref.shape[0] // dil
    h = hs_ref[...]
    for idx, (o_ref, transposed) in enumerate(((q_ref, True), (k_ref, False), (v_ref, True))):
        lo = idx * BRANCH_W
        acc = jnp.dot(h, w_ref[:, lo:lo + BRANCH_W], preferred_element_type=F32)
        acc = acc * cs_ref[:, lo:lo + BRANCH_W]
        for j in range(dil):
            part = acc[j * sub:(j + 1) * sub, :]
            if o_ref is v_ref and v_ones:
                pt = part.T.astype(BF16)
                for p in range(BRANCH_W // PAIR_W):
                    o_ref[j, p * VT_ROWS:p * VT_ROWS + PAIR_W, :] = pt[p * PAIR_W:(p + 1) * PAIR_W, :]
                    o_ref[j, p * VT_ROWS + PAIR_W:(p + 1) * VT_ROWS, :] = jnp.ones((ONES_ROWS, sub), BF16)
            else:
                o_ref[j] = (part.T if transposed else part).astype(o_ref.dtype)


def _project(x, nw, w, colscale, dil, v_ones=False):
    B, S, D = x.shape
    L = S // dil
    R = max(PROJ_ROWS, LANES * dil)
    R = min(R, S)
    sub = R // dil
    N = w.shape[1]
    vrows = (BRANCH_W // PAIR_W) * VT_ROWS if v_ones else BRANCH_W
    t_shape = jax.ShapeDtypeStruct((B, dil, BRANCH_W, L), BF16)
    n_shape = jax.ShapeDtypeStruct((B, dil, L, BRANCH_W), BF16)
    v_shape = jax.ShapeDtypeStruct((B, dil, vrows, L), BF16)
    t_spec = pl.BlockSpec((None, dil, BRANCH_W, sub), lambda b, i: (b, 0, 0, i))
    n_spec = pl.BlockSpec((None, dil, sub, BRANCH_W), lambda b, i: (b, 0, i, 0))
    v_spec = pl.BlockSpec((None, dil, vrows, sub), lambda b, i: (b, 0, 0, i))
    return pl.pallas_call(
        functools.partial(_proj_kernel, dil=dil, v_ones=v_ones),
        grid=(B, S // R),
        in_specs=[pl.BlockSpec((None, R, D), lambda b, i: (b, i, 0)),
                  _const_spec((1, D)), _const_spec((D, N)), _const_spec((1, N))],
        out_specs=[t_spec, n_spec, v_spec],
        out_shape=[t_shape, n_shape, v_shape],
        scratch_shapes=[pltpu.VMEM((R, D), BF16),
                        pltpu.VMEM((D // LANES, R if dil > 1 else 8, LANES), F32)],
        compiler_params=_cparams("parallel", "parallel"),
        name=f"proj_dil{dil}",
    )(x, nw, w, colscale)


def _norm_rope_t(yt, g, cos, sg):
    outs = []
    for h in range(yt.shape[0] // HEAD_DIM):
        yh = yt[h * HEAD_DIM:(h + 1) * HEAD_DIM, :]
        yn = yh * lax.rsqrt(jnp.mean(yh * yh, axis=0, keepdims=True) + EPS) * g
        swapped = jnp.concatenate([yn[16:32], yn[0:16], yn[48:64], yn[32:48]], axis=0)
        outs.append(yn * cos + swapped * sg)
    return jnp.concatenate(outs, axis=0)


def _proj_d_kernel(x_ref, nw_ref, w_ref, gq_ref, gk_ref, cos_ref, sg_ref, q_ref, k_ref, v_ref, hs_ref):
    _normed_rows(x_ref, nw_ref, hs_ref, None, 1)
    h = hs_ref[...]
    cos = cos_ref[...]
    sg = sg_ref[...]
    qw = D_HEADS * HEAD_DIM
    kw = D_KV_HEADS * HEAD_DIM
    qt = jnp.dot(h, w_ref[:, 0:qw], preferred_element_type=F32).T
    q_ref[...] = (_norm_rope_t(qt, gq_ref[...], cos, sg) * (QK_SCALE * LOG2E)).astype(BF16)
    kt = jnp.dot(h, w_ref[:, qw:qw + kw], preferred_element_type=F32).T
    k_ref[...] = _norm_rope_t(kt, gk_ref[...], cos, sg).T.astype(BF16)
    v_ref[0:kw, :] = jnp.dot(h, w_ref[:, qw + kw:qw + 2 * kw], preferred_element_type=F32).T.astype(BF16)
    v_ref[kw:, :] = jnp.ones((ONES_ROWS, v_ref.shape[1]), BF16)


def _project_d(x, nw, w, gq, gk, cos_t, sg_t):
    B, S, D = x.shape
    R = min(PROJ_ROWS, S)
    qw = D_HEADS * HEAD_DIM
    kw = D_KV_HEADS * HEAD_DIM
    return pl.pallas_call(
        _proj_d_kernel,
        grid=(B, S // R),
        in_specs=[pl.BlockSpec((None, R, D), lambda b, i: (b, i, 0)),
                  _const_spec((1, D)), _const_spec((D, qw + 2 * kw)),
                  _const_spec((HEAD_DIM, 1)), _const_spec((HEAD_DIM, 1)),
                  pl.BlockSpec((HEAD_DIM, R), lambda b, i: (0, i)),
                  pl.BlockSpec((HEAD_DIM, R), lambda b, i: (0, i))],
        out_specs=[pl.BlockSpec((None, qw, R), lambda b, i: (b, 0, i)),
                   pl.BlockSpec((None, R, kw), lambda b, i: (b, i, 0)),
                   pl.BlockSpec((None, VT_ROWS, R), lambda b, i: (b, 0, i))],
        out_shape=[jax.ShapeDtypeStruct((B, qw, S), BF16),
                   jax.ShapeDtypeStruct((B, S, kw), BF16),
                   jax.ShapeDtypeStruct((B, VT_ROWS, S), BF16)],
        scratch_shapes=[pltpu.VMEM((R, D), BF16)],
        compiler_params=_cparams("parallel", "parallel"),
        name="proj_gqa",
    )(x, nw, w, gq, gk, cos_t, sg_t)


FLASH_COLS = 256


def _group_spans(c0, width):
    spans, c = [], c0
    while c < c0 + width:
        g, lo = divmod(c, FLASH_COLS)
        n = min(FLASH_COLS - lo, c0 + width - c)
        spans.append((g, lo, n))
        c += n
    return spans


def _read_cols(ref, rows, c0, width):
    parts = [ref[g, rows, lo:lo + n] for g, lo, n in _group_spans(c0, width)]
    return parts[0] if len(parts) == 1 else jnp.concatenate(parts, axis=1)


def _stack_pair_queries(q2_ref, qt_ref, nchunk, tq):
    row = lax.broadcasted_iota(jnp.int32, (PAIR_W, tq), 0)
    for c in range(nchunk):
        qc = qt_ref[c * PAIR_W:(c + 1) * PAIR_W, :].astype(F32)
        for half, keep in enumerate((row < HEAD_DIM, row >= HEAD_DIM)):
            val = jnp.where(keep, qc, 0.0).astype(BF16)
            off = 0
            for g, lo, n in _group_spans((2 * c + half) * tq, tq):
                q2_ref[g, :, lo:lo + n] = val[:, off:off + n]
                off += n


def _flash_scratch(M, tk):
    G = M // FLASH_COLS
    stat = pltpu.VMEM((G, 1, FLASH_COLS), F32)
    probs = pltpu.VMEM((G, tk, FLASH_COLS), BF16)
    return [pltpu.VMEM((G, PAIR_W, FLASH_COLS), BF16), stat, stat,
            pltpu.VMEM((G, VT_ROWS, FLASH_COLS), F32), probs,
            pltpu.VMEM((G, tk, FLASH_COLS), F32), pltpu.VMEM((G, tk, FLASH_COLS), F32), stat, stat,
            probs, stat, stat]


def _flash_loop(k_ref, vt_ref, q2_ref, m_ref, a_ref, acc_ref, p_ref, sa_ref, sb_ref, ma_ref, mb_ref,
                *, tq, tk, seq, bias_fn):
    ngroups = q2_ref.shape[0]
    nk = seq // tk
    m_ref[...] = jnp.full(m_ref.shape, NEG_INF, F32)
    acc_ref[...] = jnp.zeros(acc_ref.shape, F32)

    def scores(j, g, s_ref, mx_ref):
        k = k_ref[pl.ds(pl.multiple_of(j * tk, tk), tk), :]
        s = jnp.dot(k, q2_ref[g], preferred_element_type=F32)
        if bias_fn is not None:
            tile, shift, _ = bias_fn(j, (g * FLASH_COLS) % tq)
            s = s + tile + shift
        s_ref[g] = s
        mx_ref[g] = jnp.max(s, axis=0, keepdims=True)

    def probs(g, s_ref, mx_ref):
        m_old = m_ref[g]
        m_new = jnp.maximum(m_old, mx_ref[g])
        m_ref[g] = m_new
        a_ref[g] = jnp.exp2(m_old - m_new)
        p_ref[g] = jnp.exp2((s_ref[g] - m_new).astype(BF16))

    def values(j, g):
        vt = vt_ref[:, pl.ds(pl.multiple_of(j * tk, tk), tk)]
        acc_ref[g] = acc_ref[g] * a_ref[g] + jnp.dot(vt, p_ref[g], preferred_element_type=F32)

    def step(j, cur, nxt):
        for g in range(ngroups):
            if nxt is not None:
                scores(j + 1, g, *nxt)
            probs(g, *cur)
            if g > 0:
                values(j, g - 1)
        values(j, ngroups - 1)

    buf_a, buf_b = (sa_ref, ma_ref), (sb_ref, mb_ref)
    for g in range(ngroups):
        scores(0, g, *buf_a)

    def body(i, carry):
        step(2 * i, buf_a, buf_b)
        step(2 * i + 1, buf_b, buf_a)
        return carry

    lax.fori_loop(0, nk // 2 - 1, body, 0)
    step(nk - 2, buf_a, buf_b)
    step(nk - 1, buf_b, None)


LAG_LIMIT = 64.0
FLASH_PROBE_ROWS = 128


def _flash_attend(k_ref, vt_ref, q2_ref, m_ref, r_ref, acc_ref, pa_ref, sa_ref, sb_ref, ra_ref, rb_ref,
                  pb_ref, lag_ref, mu_ref, *, tq, tk, seq, bias_fn):
    ngroups = q2_ref.shape[0]
    nk = seq // tk

    def block_scores(j, g, rows=tk):
        k = k_ref[pl.ds(pl.multiple_of(j * tk, tk), rows), :]
        s = jnp.dot(k, q2_ref[g], preferred_element_type=F32)
        if bias_fn is None:
            return s, 0.0, 0.0
        tile, shift, bias_max = bias_fn(j, (g * FLASH_COLS) % tq, rows)
        return s + tile, shift, bias_max

    for g in range(ngroups):
        s, shift, bias_max = block_scores(0, g, FLASH_PROBE_ROWS)
        first = jnp.max(s, axis=0, keepdims=True) + shift
        m_ref[g] = first
        mu_ref[g] = first - bias_max
        r_ref[g] = first
    acc_ref[...] = jnp.zeros(acc_ref.shape, F32)
    lag_ref[...] = jnp.zeros(lag_ref.shape, F32)

    def scores(j, g, p_ref, rp_ref):
        s, shift, bias_max = block_scores(j, g)
        m_old = m_ref[g]
        ref = jnp.maximum(m_old, mu_ref[g] + bias_max)
        rp_ref[g] = ref
        p_ref[g] = jnp.exp2((s - (ref - shift)).astype(BF16))
        mx = jnp.max(s, axis=0, keepdims=True) + shift
        m_new = jnp.maximum(m_old, mx)
        lag_ref[g] = jnp.maximum(lag_ref[g], jnp.maximum(mx - ref, ref - m_new))
        m_ref[g] = m_new
        mu_ref[g] = jnp.maximum(mu_ref[g], mx - bias_max)

    def values(j, g, p_ref, rp_ref):
        vt = vt_ref[:, pl.ds(pl.multiple_of(j * tk, tk), tk)]
        ref = rp_ref[g]
        acc_ref[g] = (acc_ref[g] * jnp.exp2(r_ref[g] - ref)
                      + jnp.dot(vt, p_ref[g], preferred_element_type=F32))
        r_ref[g] = ref

    def step(j, cur, nxt):
        for g in range(ngroups):
            if nxt is not None:
                scores(j + 1, g, *nxt)
            values(j, g, *cur)

    buf_a, buf_b = (pa_ref, ra_ref), (pb_ref, rb_ref)
    for g in range(ngroups):
        scores(0, g, *buf_a)

    def body(i, carry):
        step(2 * i, buf_a, buf_b)
        step(2 * i + 1, buf_b, buf_a)
        return carry

    lax.fori_loop(0, nk // 2 - 1, body, 0)
    step(nk - 2, buf_a, buf_b)
    step(nk - 1, buf_b, None)

    @pl.when(jnp.max(lag_ref[...]) > LAG_LIMIT)
    def _():
        _flash_loop(k_ref, vt_ref, q2_ref, m_ref, r_ref, acc_ref, pa_ref, sa_ref, sb_ref, ra_ref, rb_ref,
                    tq=tq, tk=tk, seq=seq, bias_fn=bias_fn)


def _flash_a_kernel(slope_ref, lam_ref, g_ref, qt_ref, k_ref, vt_ref, o_ref,
                    q2_ref, m_ref, a_ref, acc_ref, *bufs, tq, tk, seq, lam_init):
    _stack_pair_queries(q2_ref, qt_ref, 1, tq)
    *bufs, tiles_ref = bufs
    negc2 = -LOG2E * slope_ref[pl.program_id(1)]
    q0 = pl.program_id(2) * tq
    @pl.when(pl.program_id(2) == 0)
    def _():
        d0 = (lax.broadcasted_iota(jnp.int32, (tk, FLASH_COLS), 1)
              - lax.broadcasted_iota(jnp.int32, (tk, FLASH_COLS), 0)).astype(F32)
        tiles_ref[0] = d0 * negc2
        tiles_ref[1] = d0 * (-negc2)
        for n in range(tk // FLASH_COLS):
            tiles_ref[2 + n] = jnp.abs(d0 + float(n * FLASH_COLS)) * negc2

    qcol = lax.broadcasted_iota(jnp.int32, (1, FLASH_COLS), 1).astype(F32)

    def bias_fn(j, qcol0, rows=tk):
        delta = q0 + qcol0 - j * tk
        right, left = delta >= tk, delta <= -FLASH_COLS
        idx = jnp.where(right, 0, jnp.where(left, 1, 2 + jnp.clip(delta // FLASH_COLS, 0, tk // FLASH_COLS - 1)))
        shift = jnp.where(right | left, negc2 * jnp.abs(delta).astype(F32), 0.0)
        t = qcol + delta.astype(F32)
        nearest = jnp.maximum(jnp.maximum(t - float(rows - 1), -t), 0.0)
        return tiles_ref[idx, 0:rows, :], shift, negc2 * nearest

    _flash_attend(k_ref, vt_ref, q2_ref, m_ref, a_ref, acc_ref, *bufs,
                tq=tq, tk=tk, seq=seq, bias_fn=bias_fn)

    lp = lam_ref[...]
    lam = (jnp.exp(jnp.sum(lp[0:1] * lp[1:2], axis=1, keepdims=True))
           - jnp.exp(jnp.sum(lp[2:3] * lp[3:4], axis=1, keepdims=True)) + lam_init)
    vals, sums = slice(0, PAIR_W), slice(PAIR_W, PAIR_W + 1)
    o = (_read_cols(acc_ref, vals, 0, tq) / _read_cols(acc_ref, sums, 0, tq)
         - lam * (_read_cols(acc_ref, vals, tq, tq) / _read_cols(acc_ref, sums, tq, tq)))
    o = o * lax.rsqrt(jnp.mean(o * o, axis=0, keepdims=True) + EPS) * g_ref[...]
    o_ref[...] = (o * (1.0 - lam_init)).T.astype(o_ref.dtype)


def _flash_a(slopes, lamp, subg, qt, k, vt, lam_init, tq, tk):
    B, _, S = qt.shape
    tq, tk = min(tq, S), min(tk, S)
    M = 2 * tq
    return pl.pallas_call(
        functools.partial(_flash_a_kernel, tq=tq, tk=tk, seq=S, lam_init=lam_init),
        grid=(B, A_HEADS, S // tq),
        in_specs=[pl.BlockSpec(memory_space=pltpu.SMEM),
                  _const_spec((4, HEAD_DIM)), _const_spec((PAIR_W, 1)),
                  pl.BlockSpec((None, PAIR_W, tq), lambda b, h, i: (b, h, i)),
                  pl.BlockSpec((None, S, PAIR_W), lambda b, h, i: (b, 0, h)),
                  pl.BlockSpec((None, VT_ROWS, S), lambda b, h, i: (b, h, 0))],
        out_specs=pl.BlockSpec((None, tq, PAIR_W), lambda b, h, i: (b, i, h)),
        out_shape=jax.ShapeDtypeStruct((B, S, A_HEADS * PAIR_W), BF16),
        scratch_shapes=_flash_scratch(M, tk) + [pltpu.VMEM((2 + tk // FLASH_COLS, tk, FLASH_COLS), F32)],
        compiler_params=_cparams("parallel", "parallel", "arbitrary"),
        name="flash_diff",
    )(slopes, lamp, subg, qt, k, vt)


def _flash_d_kernel(qt_ref, k_ref, vt_ref, o_ref, q2_ref, m_ref, a_ref, acc_ref, *bufs, tq, tk, seq):
    nchunk = qt_ref.shape[0] // PAIR_W
    _stack_pair_queries(q2_ref, qt_ref, nchunk, tq)
    _flash_attend(k_ref, vt_ref, q2_ref, m_ref, a_ref, acc_ref, *bufs,
                tq=tq, tk=tk, seq=seq, bias_fn=None)
    sums = slice(PAIR_W, PAIR_W + 1)
    for c in range(nchunk):
        top, bot = (2 * c) * tq, (2 * c + 1) * tq
        ot = jnp.concatenate(
            [_read_cols(acc_ref, slice(0, HEAD_DIM), top, tq) / _read_cols(acc_ref, sums, top, tq),
             _read_cols(acc_ref, slice(HEAD_DIM, PAIR_W), bot, tq) / _read_cols(acc_ref, sums, bot, tq)],
            axis=0)
        o_ref[:, c * PAIR_W:(c + 1) * PAIR_W] = ot.T.astype(o_ref.dtype)


def _flash_d(qt, k, vt, tq, tk):
    B, qw, S = qt.shape
    tq, tk = min(tq, S), min(tk, S)
    M = 2 * (qw // PAIR_W) * tq
    return pl.pallas_call(
        functools.partial(_flash_d_kernel, tq=tq, tk=tk, seq=S),
        grid=(B, S // tq),
        in_specs=[pl.BlockSpec((None, qw, tq), lambda b, i: (b, 0, i)),
                  pl.BlockSpec((None, S, PAIR_W), lambda b, i: (b, 0, 0)),
                  pl.BlockSpec((None, VT_ROWS, S), lambda b, i: (b, 0, 0))],
        out_specs=pl.BlockSpec((None, tq, qw), lambda b, i: (b, i, 0)),
        out_shape=jax.ShapeDtypeStruct((B, S, qw), BF16),
        scratch_shapes=_flash_scratch(M, tk),
        compiler_params=_cparams("parallel", "parallel"),
        name="flash_gqa",
    )(qt, k, vt)


WIN_TQ = 128
WIN_SCORE_VREGS = 1536


def _win_kernel(tab_ref, qt_ref, k_ref, vt_ref, bias_ref, o_ref, *lse_refs, width, nblocks, unroll):
    row = lax.broadcasted_iota(jnp.int32, (PAIR_W, WIN_TQ), 0)
    pairs = qt_ref.shape[0] // PAIR_W

    def scores(pi, i):
        lanes = slice(pi * PAIR_W, (pi + 1) * PAIR_W)
        ws = pl.multiple_of(tab_ref[i, 0], 128)
        q0 = pl.multiple_of(i * WIN_TQ, WIN_TQ)
        qc = qt_ref[lanes, pl.ds(q0, WIN_TQ)].astype(F32)
        q2 = jnp.concatenate([jnp.where(row < HEAD_DIM, qc, 0.0),
                              jnp.where(row >= HEAD_DIM, qc, 0.0)], axis=1).astype(BF16)
        s = jnp.dot(k_ref[pl.ds(ws, width), lanes], q2, preferred_element_type=F32)
        return pi, lanes, ws, q0, s + bias_ref[tab_ref[i, 1], pi]

    def finish(lanes, q0, m, acc):
        l = acc[PAIR_W:PAIR_W + 1, :]
        rl = 1.0 / l
        ot = jnp.concatenate([acc[0:HEAD_DIM, 0:WIN_TQ] * rl[:, 0:WIN_TQ],
                              acc[HEAD_DIM:PAIR_W, WIN_TQ:] * rl[:, WIN_TQ:]], axis=0)
        o_ref[pl.ds(q0, WIN_TQ), lanes] = ot.T.astype(o_ref.dtype)
        if lse_refs:
            lse = m + jnp.log2(l)
            lt = jnp.concatenate([jnp.broadcast_to(lse[:, 0:WIN_TQ], (HEAD_DIM, WIN_TQ)),
                                  jnp.broadcast_to(lse[:, WIN_TQ:], (HEAD_DIM, WIN_TQ))], axis=0)
            lse_refs[0][pl.ds(q0, WIN_TQ), lanes] = lt.T

    def body(ib, carry):
        blocks = [scores(pi, ib * unroll + u) for pi in range(pairs) for u in range(unroll)]
        maxes, probs, accs = [], [], []

        def values(n):
            pi, ws = blocks[n][0], blocks[n][2]
            vt = vt_ref[pi * VT_ROWS:(pi + 1) * VT_ROWS, pl.ds(ws, width)]
            accs.append(jnp.dot(vt, probs[n], preferred_element_type=F32))

        for n, (*_, s) in enumerate(blocks):
            m = jnp.max(s, axis=0, keepdims=True)
            maxes.append(m)
            probs.append(jnp.exp2((s - m).astype(BF16)))
            if n > 0:
                values(n - 1)
        values(len(blocks) - 1)
        for (_, lanes, _, q0, _), m, acc in zip(blocks, maxes, accs):
            finish(lanes, q0, m, acc)
        return carry

    lax.fori_loop(0, nblocks // unroll, body, 0)


def _windowed(tab, qt, k, vt, bias, out_dtype, want_lse):
    N, _, L = qt.shape
    nvar, npair, width, _ = bias.shape
    nblocks = L // WIN_TQ
    in_flight = WIN_SCORE_VREGS // (width * 2 * WIN_TQ // VREG_ELEMS)
    in_flight = 1 << (in_flight.bit_length() - 1)
    assert nblocks % min(in_flight, nblocks) == 0
    unroll = min(in_flight, nblocks)
    pp = min(npair, in_flight // unroll)
    o_spec = pl.BlockSpec((None, L, pp * PAIR_W), lambda n, p: (n, 0, p))
    out_shape = [jax.ShapeDtypeStruct((N, L, npair * PAIR_W), out_dtype)]
    out_specs = [o_spec]
    if want_lse:
        out_shape.append(jax.ShapeDtypeStruct((N, L, npair * PAIR_W), F32))
        out_specs.append(o_spec)
    return pl.pallas_call(
        functools.partial(_win_kernel, width=width, nblocks=nblocks, unroll=unroll),
        grid=(N, npair // pp),
        in_specs=[pl.BlockSpec(memory_space=pltpu.SMEM),
                  pl.BlockSpec((None, pp * PAIR_W, L), lambda n, p: (n, p, 0)),
                  pl.BlockSpec((None, L, pp * PAIR_W), lambda n, p: (n, 0, p)),
                  pl.BlockSpec((None, pp * VT_ROWS, L), lambda n, p: (n, p, 0)),
                  pl.BlockSpec((nvar, pp, width, 2 * WIN_TQ), lambda n, p: (0, p, 0, 0))],
        out_specs=out_specs,
        out_shape=out_shape,
        compiler_params=_cparams("parallel", "parallel"),
        name=f"windowed_w{width}",
    )(tab, qt, k, vt, bias)


def _bmerge_kernel(o0_ref, l0_ref, o1_ref, l1_ref, o2_ref, l2_ref, y_ref, so1, sl1, so2, sl2, *, dils):
    T, width = y_ref.shape
    nchunk = width // LANES
    for (o_ref, l_ref, so, sl, dil) in ((o1_ref, l1_ref, so1, sl1, dils[1]), (o2_ref, l2_ref, so2, sl2, dils[2])):
        sub = T // dil
        for j in range(dil):
            for c in range(nchunk):
                so[c, pl.ds(j, sub, stride=dil), :] = o_ref[j, :, c * LANES:(c + 1) * LANES]
                sl[c, pl.ds(j, sub, stride=dil), :] = l_ref[j, :, c * LANES:(c + 1) * LANES]
    for c in range(nchunk):
        cols = slice(c * LANES, (c + 1) * LANES)
        o0, e0 = o0_ref[0, :, cols], l0_ref[0, :, cols]
        e1, e2 = sl1[c], sl2[c]
        mx = jnp.maximum(jnp.maximum(e0, e1), e2)
        w0, w1, w2 = jnp.exp2(e0 - mx), jnp.exp2(e1 - mx), jnp.exp2(e2 - mx)
        num = w0 * o0 + w1 * so1[c] + w2 * so2[c]
        y_ref[:, cols] = (num / (w0 + w1 + w2)).astype(y_ref.dtype)


def _bmerge(outs, S):
    B = outs[0][0].shape[0]
    dils = tuple(d for _, d in B_PATTERNS)
    T = min(512, S)
    args, in_specs = [], []
    for (o, e), dil in zip(outs, dils):
        spec = pl.BlockSpec((None, dil, T // dil, BRANCH_W), lambda b, t: (b, 0, t, 0))
        args += [o, e]
        in_specs += [spec, spec]
    return pl.pallas_call(
        functools.partial(_bmerge_kernel, dils=dils),
        grid=(B, S // T),
        in_specs=in_specs,
        out_specs=pl.BlockSpec((None, T, BRANCH_W), lambda b, t: (b, t, 0)),
        out_shape=jax.ShapeDtypeStruct((B, S, BRANCH_W), BF16),
        scratch_shapes=[pltpu.VMEM((BRANCH_W // LANES, T, LANES), F32)] * 4,
        compiler_params=_cparams("parallel", "parallel"),
        name="dilated_merge",
    )(*args)


def _merge_kernel(x_ref, nw_ref, ya_ref, yb_ref, yc_ref, yd_ref, wg_ref, bg_ref, wb_ref, wo_ref, o_ref):
    x = x_ref[...]
    h = _rms(x, nw_ref[...]).astype(BF16)
    merged = None
    for n, y_ref in enumerate((ya_ref, yb_ref, yc_ref, yd_ref)):
        cols = slice(n * D_MODEL, (n + 1) * D_MODEL)
        z = jnp.dot(h, wg_ref[:, cols], preferred_element_type=F32) + bg_ref[:, cols]
        gate = 1.0 / (1.0 + jnp.exp(-z))
        t = gate * jnp.dot(y_ref[...], wb_ref[n], preferred_element_type=F32)
        merged = t if merged is None else merged + t
    o_ref[...] = x + jnp.dot(merged.astype(BF16), wo_ref[...], preferred_element_type=F32)


def _merge(x2d, nw, ys, wg, bg, wb, wo, tm):
    Mrows, D = x2d.shape
    tm = min(tm, Mrows)
    row = lambda w: pl.BlockSpec((tm, w), lambda i: (i, 0))
    return pl.pallas_call(
        _merge_kernel,
        grid=(Mrows // tm,),
        in_specs=[row(D), _const_spec((1, D)), row(BRANCH_W), row(BRANCH_W), row(BRANCH_W), row(BRANCH_W),
                  _const_spec(wg.shape), _const_spec(bg.shape), _const_spec(wb.shape), _const_spec(wo.shape)],
        out_specs=row(D),
        out_shape=jax.ShapeDtypeStruct((Mrows, D), F32),
        compiler_params=_cparams("parallel"),
        name="gated_merge",
    )(x2d, nw, *ys, wg, bg, wb, wo)


FFN_CHUNK = 1024


def _ffn_kernel(x_ref, nw_ref, w1_ref, w2_ref, nf_ref, o_ref, *, final):
    x = x_ref[...]
    h = _rms(x, nw_ref[...]).astype(BF16)
    acc = x
    for c0 in range(0, D_FF, FFN_CHUNK):
        u = jnp.maximum(jnp.dot(h, w1_ref[:, c0:c0 + FFN_CHUNK], preferred_element_type=F32), 0.0)
        acc = acc + jnp.dot((u * u).astype(BF16), w2_ref[c0:c0 + FFN_CHUNK, :], preferred_element_type=F32)
    if final:
        acc = _rms(acc, nf_ref[...])
    o_ref[...] = acc


def _ffn(x2d, nw, w1, w2, nf, final, tm):
    Mrows, D = x2d.shape
    tm = min(tm, Mrows)
    row = pl.BlockSpec((tm, D), lambda i: (i, 0))
    return pl.pallas_call(
        functools.partial(_ffn_kernel, final=final),
        grid=(Mrows // tm,),
        in_specs=[row, _const_spec((1, D)), _const_spec(w1.shape), _const_spec(w2.shape), _const_spec((1, D))],
        out_specs=row,
        out_shape=jax.ShapeDtypeStruct((Mrows, D), F32),
        compiler_params=_cparams("parallel"),
        name="relu2_mlp",
    )(x2d, nw, w1, w2, nf)


def _alibi_slopes(n):
    return np.array([2.0 ** (-8.0 * (i + 1) / n) for i in range(n)], dtype=np.float32)


def _pair_tiles(t):
    nvar, nh, w, tq = t.shape
    t = t.reshape(nvar, nh // 2, 2, w, tq)
    return jnp.concatenate([t[:, :, 0], t[:, :, 1]], axis=-1)


def _dilated_tables(L, dil):
    width = 3 * WIN_TQ
    nb = L // WIN_TQ
    ws = np.clip((np.arange(nb) - 1) * WIN_TQ, 0, L - width)
    var = np.where(np.arange(nb) == 0, 0, np.where(np.arange(nb) == nb - 1, 2, 1))
    tab = np.stack([ws, var], axis=1).astype(np.int32)
    c = np.arange(width)[:, None]
    r = np.arange(WIN_TQ)[None, :]
    slopes = _alibi_slopes(B_HEADS)
    tiles = []
    for shift in (0, WIN_TQ, 2 * WIN_TQ):
        rel = c - shift - r
        valid = np.abs(rel) <= B_RADIUS
        bias = -slopes[:, None, None] * (np.abs(rel) * dil).astype(np.float32)[None] * np.float32(LOG2E)
        tiles.append(np.where(valid[None], bias, np.float32(NEG_INF)))
    return jnp.asarray(tab), _pair_tiles(jnp.asarray(np.stack(tiles).astype(np.float32)))


def _na_tables(S, rpb):
    rows = S // GRID_W
    kh = min(NA_KH, rows)
    wrows = 10
    nb = rows // 2
    r0s = 2 * np.arange(nb)
    wsr = np.clip(r0s - kh // 2, 0, rows - wrows)
    tab = np.stack([wsr * GRID_W, (r0s - wsr) // 2], axis=1).astype(np.int32)
    reps = [0, 2, 4, rows - 4, rows - 2]
    krl = np.arange(wrows)
    kc = np.arange(GRID_W)
    qrl = np.arange(2)
    qc = np.arange(GRID_W)
    qstart = np.clip(qc - NA_KW // 2, 0, GRID_W - NA_KW)
    col_ok = (kc[:, None] >= qstart[None, :]) & (kc[:, None] < qstart[None, :] + NA_KW)
    ridx, valid = [], []
    for r0 in reps:
        w0 = int(np.clip(r0 - kh // 2, 0, rows - wrows))
        r = (r0 + qrl)[None, :]
        rs = np.clip(r - kh // 2, 0, rows - kh)
        kr = (w0 + krl)[:, None]
        row_ok = (kr >= rs) & (kr < rs + kh)
        ridx.append(np.clip(kr - r + NA_KH - 1, 0, 2 * NA_KH - 2))
        valid.append(row_ok[:, None, :, None] & col_ok[None, :, None, :])
    ridx, valid = np.stack(ridx), np.stack(valid)
    nh = rpb.shape[0]
    span = GRID_W - NA_KW
    padded = jnp.pad(rpb.astype(F32), ((0, 0), (0, 0), (span, span)), mode="edge")
    period = 2 * GRID_W
    padded = jnp.pad(padded, ((0, 0), (0, 0), (0, 1)))
    shifted = jnp.tile(padded, (1, 1, GRID_W))[:, :, :GRID_W * (period - 1)]
    shifted = shifted.reshape(nh, -1, GRID_W, period - 1)
    by_col = jnp.swapaxes(shifted[:, :, :, GRID_W - 1:period - 1], -1, -2)
    slabs = jnp.take(by_col, jnp.asarray(ridx.reshape(-1)), axis=1)
    slabs = slabs.reshape(nh, len(reps), wrows, 2, GRID_W, GRID_W).transpose(1, 0, 2, 4, 3, 5)
    tiles = jnp.where(valid[:, None], slabs * LOG2E, NEG_INF)
    tiles = tiles.reshape(len(reps), nh, wrows * GRID_W, 2 * GRID_W)
    return jnp.asarray(tab), _pair_tiles(tiles)


def _rope_tables(S):
    n = HEAD_DIM // 4
    pos = np.arange(S)
    inv = ROPE_THETA ** (-np.arange(n, dtype=np.float32) / n)
    d = np.arange(HEAD_DIM)
    p = np.where((d < HEAD_DIM // 2)[:, None], (pos // GRID_W)[None, :], (pos % GRID_W)[None, :]).astype(np.float32)
    ang = p * inv[d % n][:, None]
    sign = np.where((d % (2 * n)) < n, -1.0, 1.0).astype(np.float32)[:, None]
    return jnp.asarray(np.cos(ang).astype(np.float32)), jnp.asarray((np.sin(ang) * sign).astype(np.float32))


def kernel(x, norm_mix, w_in, b_gate, diff_lambda, diff_subln, na_rpb, qk_norm, w_branch, w_out, norm_ffn,
           w_ff1, w_ff2, norm_final):
    B, S, D = x.shape
    depth = w_in.shape[0]
    rows = B * S

    ones_col = jnp.ones((1, 3 * BRANCH_W), F32)
    qscale_col = ones_col.at[:, 0:BRANCH_W].set(QK_SCALE * LOG2E)
    a_slopes = jnp.asarray(_alibi_slopes(A_HEADS))
    cos_t, sg_t = _rope_tables(S)
    b_tabs = [_dilated_tables(S // dil, dil) for _, dil in B_PATTERNS]
    d_perm = np.array([(g * 4 + c) * HEAD_DIM + d for c in range(4) for g in range(D_KV_HEADS)
                       for d in range(HEAD_DIM)])

    o_a, o_b, o_c, o_d, o_gate = 0, 1536, 6144, 7680, 8448
    for l in range(depth):
        w = w_in[l].astype(BF16)
        nw = norm_mix[l].reshape(1, D)
        lam_init = 0.8 - 0.6 * math.exp(-0.3 * l)

        aq, ak, av = _project(x, nw, w[:, o_a:o_a + 1536], qscale_col, 1, v_ones=True)
        y_a = _flash_a(a_slopes, diff_lambda[l], diff_subln[l].reshape(PAIR_W, 1),
                       aq[:, 0], ak[:, 0], av[:, 0], lam_init, FLASH_TQ_DIFF, FLASH_TK)

        b_outs = []
        for g, (_, dil) in enumerate(B_PATTERNS):
            lo = o_b + g * 1536
            bq, bk, bv = _project(x, nw, w[:, lo:lo + 1536], qscale_col, dil, v_ones=True)
            L = S // dil
            tab, bias = b_tabs[g]
            o, e = _windowed(tab, bq.reshape(B * dil, BRANCH_W, L), bk.reshape(B * dil, L, BRANCH_W),
                             bv.reshape(B * dil, -1, L), bias, F32, True)
            b_outs.append((o.reshape(B, dil, L, BRANCH_W), e.reshape(B, dil, L, BRANCH_W)))
        y_b = _bmerge(b_outs, S)

        cq, ck, cv = _project(x, nw, w[:, o_c:o_c + 1536], qscale_col, 1, v_ones=True)
        c_tab, c_bias = _na_tables(S, na_rpb[l])
        (y_c,) = _windowed(c_tab, cq[:, 0], ck[:, 0], cv[:, 0], c_bias, BF16, False)

        wd = jnp.concatenate([w[:, o_d:o_d + 512][:, d_perm], w[:, o_d + 512:o_gate]], axis=1)
        dq, dk, dv = _project_d(x, nw, wd, qk_norm[l, 0].reshape(HEAD_DIM, 1), qk_norm[l, 1].reshape(HEAD_DIM, 1),
                                cos_t, sg_t)
        y_d = _flash_d(dq, dk, dv, FLASH_TQ_GQA, FLASH_TK)

        wb = w_branch[l].astype(BF16)
        wb = wb.at[3].set(wb[3][d_perm, :])
        ys = [y.reshape(rows, BRANCH_W) for y in (y_a, y_b, y_c, y_d)]
        x2d = _merge(x.reshape(rows, D), nw, ys, w[:, o_gate:], b_gate[l].reshape(1, -1), wb,
                     w_out[l].astype(BF16), DENSE_ROWS)

        x2d = _ffn(x2d, norm_ffn[l].reshape(1, D), w_ff1[l].astype(BF16), w_ff2[l].astype(BF16),
                   norm_final.reshape(1, D), l == depth - 1, DENSE_ROWS)
        x = x2d.reshape(B, S, D)
    return x
```

```python
import functools
import math

import numpy as np
import jax
import jax.numpy as jnp
from jax import lax
from jax.experimental import pallas as pl
from jax.experimental.pallas import tpu as pltpu

F32 = jnp.float32
BF16 = jnp.bfloat16

D_MODEL = 1024
GRID_W = 64
HEAD_DIM = 64
PAIR_W = 2 * HEAD_DIM
N_BRANCH = 4
BRANCH_W = 512
A_HEADS = 4
B_PATTERNS = ((128, 1), (512, 4), (2048, 16))
B_HEADS = 8
B_RADIUS = 64
C_HEADS = 8
NA_KH = 8
NA_KW = 16
D_HEADS = 8
D_KV_HEADS = 2
ROPE_THETA = 10000.0
D_FF = 4 * D_MODEL
EPS = 1e-6
NEG_INF = -1e30
LOG2E = math.log2(math.e)
QK_SCALE = HEAD_DIM ** -0.5

VMEM_LIMIT_BYTES = 56 * 1024 * 1024
VREG_ELEMS = 8 * 128

FLASH_TQ_DIFF = 1024
FLASH_TQ_GQA = 256
FLASH_TK = 1024
PROJ_ROWS = 1024
DENSE_ROWS = 512


def _cparams(*sem):
    return pltpu.CompilerParams(dimension_semantics=sem, vmem_limit_bytes=VMEM_LIMIT_BYTES)


def _const_spec(shape):
    nd = len(shape)
    return pl.BlockSpec(shape, lambda *_: (0,) * nd)


def _rms(x, g):
    return x * lax.rsqrt(jnp.mean(x * x, axis=-1, keepdims=True) + EPS) * g


LANES = 128


def _normed_rows(x_ref, nw_ref, hs_ref, xs_ref, dil):
    rows, width = x_ref.shape
    sub = rows // dil
    h = _rms(x_ref[...], nw_ref[...])
    if dil == 1:
        hs_ref[...] = h.astype(BF16)
        return
    for c in range(width // LANES):
        xs_ref[c] = h[:, c * LANES:(c + 1) * LANES]
    for j in range(dil):
        for c in range(width // LANES):
            hs_ref[j * sub:(j + 1) * sub, c * LANES:(c + 1) * LANES] = (
                xs_ref[c, pl.ds(j, sub, stride=dil), :].astype(BF16))


ONES_ROWS = 16
VT_ROWS = PAIR_W + ONES_ROWS


def _proj_kernel(x_ref, nw_ref, w_ref, cs_ref, q_ref, k_ref, v_ref, hs_ref, xs_ref, *, dil, v_ones):
    _normed_rows(x_ref, nw_ref, hs_ref, xs_ref, dil)
    sub = x_ref.shape[0] // dil
    h = hs_ref[...]
    for idx, (o_ref, transposed) in enumerate(((q_ref, True), (k_ref, False), (v_ref, True))):
        lo = idx * BRANCH_W
        acc = jnp.dot(h, w_ref[:, lo:lo + BRANCH_W], preferred_element_type=F32)
        acc = acc * cs_ref[:, lo:lo + BRANCH_W]
        for j in range(dil):
            part = acc[j * sub:(j + 1) * sub, :]
            if o_ref is v_ref and v_ones:
                pt = part.T.astype(BF16)
                for p in range(BRANCH_W // PAIR_W):
                    o_ref[j, p * VT_ROWS:p * VT_ROWS + PAIR_W, :] = pt[p * PAIR_W:(p + 1) * PAIR_W, :]
                    o_ref[j, p * VT_ROWS + PAIR_W:(p + 1) * VT_ROWS, :] = jnp.ones((ONES_ROWS, sub), BF16)
            else:
                o_ref[j] = (part.T if transposed else part).astype(o_ref.dtype)


def _project(x, nw, w, colscale, dil, v_ones=False):
    B, S, D = x.shape
    L = S // dil
    R = max(PROJ_ROWS, LANES * dil)
    R = min(R, S)
    sub = R // dil
    N = w.shape[1]
    vrows = (BRANCH_W // PAIR_W) * VT_ROWS if v_ones else BRANCH_W
    t_shape = jax.ShapeDtypeStruct((B, dil, BRANCH_W, L), BF16)
    n_shape = jax.ShapeDtypeStruct((B, dil, L, BRANCH_W), BF16)
    v_shape = jax.ShapeDtypeStruct((B, dil, vrows, L), BF16)
    t_spec = pl.BlockSpec((None, dil, BRANCH_W, sub), lambda b, i: (b, 0, 0, i))
    n_spec = pl.BlockSpec((None, dil, sub, BRANCH_W), lambda b, i: (b, 0, i, 0))
    v_spec = pl.BlockSpec((None, dil, vrows, sub), lambda b, i: (b, 0, 0, i))
    return pl.pallas_call(
        functools.partial(_proj_kernel, dil=dil, v_ones=v_ones),
        grid=(B, S // R),
        in_specs=[pl.BlockSpec((None, R, D), lambda b, i: (b, i, 0)),
                  _const_spec((1, D)), _const_spec((D, N)), _const_spec((1, N))],
        out_specs=[t_spec, n_spec, v_spec],
        out_shape=[t_shape, n_shape, v_shape],
        scratch_shapes=[pltpu.VMEM((R, D), BF16),
                        pltpu.VMEM((D // LANES, R if dil > 1 else 8, LANES), F32)],
        compiler_params=_cparams("parallel", "parallel"),
        name=f"proj_dil{dil}",
    )(x, nw, w, colscale)


def _norm_rope_t(yt, g, cos, sg):
    outs = []
    for h in range(yt.shape[0] // HEAD_DIM):
        yh = yt[h * HEAD_DIM:(h + 1) * HEAD_DIM, :]
        yn = yh * lax.rsqrt(jnp.mean(yh * yh, axis=0, keepdims=True) + EPS) * g
        swapped = jnp.concatenate([yn[16:32], yn[0:16], yn[48:64], yn[32:48]], axis=0)
        outs.append(yn * cos + swapped * sg)
    return jnp.concatenate(outs, axis=0)


def _proj_d_kernel(x_ref, nw_ref, w_ref, gq_ref, gk_ref, cos_ref, sg_ref, q_ref, k_ref, v_ref, hs_ref):
    _normed_rows(x_ref, nw_ref, hs_ref, None, 1)
    h = hs_ref[...]
    cos = cos_ref[...]
    sg = sg_ref[...]
    qw = D_HEADS * HEAD_DIM
    kw = D_KV_HEADS * HEAD_DIM
    qt = jnp.dot(h, w_ref[:, 0:qw], preferred_element_type=F32).T
    q_ref[...] = (_norm_rope_t(qt, gq_ref[...], cos, sg) * (QK_SCALE * LOG2E)).astype(BF16)
    kt = jnp.dot(h, w_ref[:, qw:qw + kw], preferred_element_type=F32).T
    k_ref[...] = _norm_rope_t(kt, gk_ref[...], cos, sg).T.astype(BF16)
    v_ref[0:kw, :] = jnp.dot(h, w_ref[:, qw + kw:qw + 2 * kw], preferred_element_type=F32).T.astype(BF16)
    v_ref[kw:, :] = jnp.ones((ONES_ROWS, v_ref.shape[1]), BF16)


def _project_d(x, nw, w, gq, gk, cos_t, sg_t):
    B, S, D = x.shape
    R = min(PROJ_ROWS, S)
    qw = D_HEADS * HEAD_DIM
    kw = D_KV_HEADS * HEAD_DIM
    return pl.pallas_call(
        _proj_d_kernel,
        grid=(B, S // R),
        in_specs=[pl.BlockSpec((None, R, D), lambda b, i: (b, i, 0)),
                  _const_spec((1, D)), _const_spec((D, qw + 2 * kw)),
                  _const_spec((HEAD_DIM, 1)), _const_spec((HEAD_DIM, 1)),
                  pl.BlockSpec((HEAD_DIM, R), lambda b, i: (0, i)),
                  pl.BlockSpec((HEAD_DIM, R), lambda b, i: (0, i))],
        out_specs=[pl.BlockSpec((None, qw, R), lambda b, i: (b, 0, i)),
                   pl.BlockSpec((None, R, kw), lambda b, i: (b, i, 0)),
                   pl.BlockSpec((None, VT_ROWS, R), lambda b, i: (b, 0, i))],
        out_shape=[jax.ShapeDtypeStruct((B, qw, S), BF16),
                   jax.ShapeDtypeStruct((B, S, kw), BF16),
                   jax.ShapeDtypeStruct((B, VT_ROWS, S), BF16)],
        scratch_shapes=[pltpu.VMEM((R, D), BF16)],
        compiler_params=_cparams("parallel", "parallel"),
        name="proj_gqa",
    )(x, nw, w, gq, gk, cos_t, sg_t)


FLASH_COLS = 256


def _group_spans(c0, width):
    spans, c = [], c0
    while c < c0 + width:
        g, lo = divmod(c, FLASH_COLS)
        n = min(FLASH_COLS - lo, c0 + width - c)
        spans.append((g, lo, n))
        c += n
    return spans


def _read_cols(ref, rows, c0, width):
    parts = [ref[g, rows, lo:lo + n] for g, lo, n in _group_spans(c0, width)]
    return parts[0] if len(parts) == 1 else jnp.concatenate(parts, axis=1)


def _stack_pair_queries(q2_ref, qt_ref, nchunk, tq):
    row = lax.broadcasted_iota(jnp.int32, (PAIR_W, tq), 0)
    for c in range(nchunk):
        qc = qt_ref[c * PAIR_W:(c + 1) * PAIR_W, :].astype(F32)
        for half, keep in enumerate((row < HEAD_DIM, row >= HEAD_DIM)):
            val = jnp.where(keep, qc, 0.0).astype(BF16)
            off = 0
            for g, lo, n in _group_spans((2 * c + half) * tq, tq):
                q2_ref[g, :, lo:lo + n] = val[:, off:off + n]
                off += n


def _flash_scratch(M, tk):
    G = M // FLASH_COLS
    stat = pltpu.VMEM((G, 1, FLASH_COLS), F32)
    probs = pltpu.VMEM((G, tk, FLASH_COLS), BF16)
    return [pltpu.VMEM((G, PAIR_W, FLASH_COLS), BF16), stat, stat,
            pltpu.VMEM((G, VT_ROWS, FLASH_COLS), F32), probs,
            pltpu.VMEM((G, tk, FLASH_COLS), F32), pltpu.VMEM((G, tk, FLASH_COLS), F32), stat, stat,
            probs, stat, stat]


def _flash_loop(k_ref, vt_ref, q2_ref, m_ref, a_ref, acc_ref, p_ref, sa_ref, sb_ref, ma_ref, mb_ref,
                *, tq, tk, seq, bias_fn):
    ngroups = q2_ref.shape[0]
    nk = seq // tk
    m_ref[...] = jnp.full(m_ref.shape, NEG_INF, F32)
    acc_ref[...] = jnp.zeros(acc_ref.shape, F32)

    def scores(j, g, s_ref, mx_ref):
        k = k_ref[pl.ds(pl.multiple_of(j * tk, tk), tk), :]
        s = jnp.dot(k, q2_ref[g], preferred_element_type=F32)
        if bias_fn is not None:
            tile, shift, _ = bias_fn(j, (g * FLASH_COLS) % tq)
            s = s + tile + shift
        s_ref[g] = s
        mx_ref[g] = jnp.max(s, axis=0, keepdims=True)

    def probs(g, s_ref, mx_ref):
        m_old = m_ref[g]
        m_new = jnp.maximum(m_old, mx_ref[g])
        m_ref[g] = m_new
        a_ref[g] = jnp.exp2(m_old - m_new)
        p_ref[g] = jnp.exp2((s_ref[g] - m_new).astype(BF16))

    def values(j, g):
        vt = vt_ref[:, pl.ds(pl.multiple_of(j * tk, tk), tk)]
        acc_ref[g] = acc_ref[g] * a_ref[g] + jnp.dot(vt, p_ref[g], preferred_element_type=F32)

    def step(j, cur, nxt):
        for g in range(ngroups):
            if nxt is not None:
                scores(j + 1, g, *nxt)
            probs(g, *cur)
            if g > 0:
                values(j, g - 1)
        values(j, ngroups - 1)

    buf_a, buf_b = (sa_ref, ma_ref), (sb_ref, mb_ref)
    for g in range(ngroups):
        scores(0, g, *buf_a)

    def body(i, carry):
        step(2 * i, buf_a, buf_b)
        step(2 * i + 1, buf_b, buf_a)
        return carry

    lax.fori_loop(0, nk // 2 - 1, body, 0)
    step(nk - 2, buf_a, buf_b)
    step(nk - 1, buf_b, None)


LAG_LIMIT = 64.0
FLASH_PROBE_ROWS = 128


def _flash_attend(k_ref, vt_ref, q2_ref, m_ref, r_ref, acc_ref, pa_ref, sa_ref, sb_ref, ra_ref, rb_ref,
                  pb_ref, lag_ref, mu_ref, *, tq, tk, seq, bias_fn):
    ngroups = q2_ref.shape[0]
    nk = seq // tk

    def block_scores(j, g, rows=tk):
        k = k_ref[pl.ds(pl.multiple_of(j * tk, tk), rows), :]
        s = jnp.dot(k, q2_ref[g], preferred_element_type=F32)
        if bias_fn is None:
            return s, 0.0, 0.0
        tile, shift, bias_max = bias_fn(j, (g * FLASH_COLS) % tq, rows)
        return s + tile, shift, bias_max

    for g in range(ngroups):
        s, shift, bias_max = block_scores(0, g, FLASH_PROBE_ROWS)
        first = jnp.max(s, axis=0, keepdims=True) + shift
        m_ref[g] = first
        mu_ref[g] = first - bias_max
        r_ref[g] = first
    acc_ref[...] = jnp.zeros(acc_ref.shape, F32)
    lag_ref[...] = jnp.zeros(lag_ref.shape, F32)

    def scores(j, g, p_ref, rp_ref):
        s, shift, bias_max = block_scores(j, g)
        m_old = m_ref[g]
        ref = jnp.maximum(m_old, mu_ref[g] + bias_max)
        rp_ref[g] = ref
        p_ref[g] = jnp.exp2((s - (ref - shift)).astype(BF16))
        mx = jnp.max(s, axis=0, keepdims=True) + shift
        m_new = jnp.maximum(m_old, mx)
        lag_ref[g] = jnp.maximum(lag_ref[g], jnp.maximum(mx - ref, ref - m_new))
        m_ref[g] = m_new
        mu_ref[g] = jnp.maximum(mu_ref[g], mx - bias_max)

    def values(j, g, p_ref, rp_ref):
        vt = vt_ref[:, pl.ds(pl.multiple_of(j * tk, tk), tk)]
        ref = rp_ref[g]
        acc_ref[g] = (acc_ref[g] * jnp.exp2(r_ref[g] - ref)
                      + jnp.dot(vt, p_ref[g], preferred_element_type=F32))
        r_ref[g] = ref

    def step(j, cur, nxt):
        for g in range(ngroups):
            if nxt is not None:
                scores(j + 1, g, *nxt)
            values(j, g, *cur)

    buf_a, buf_b = (pa_ref, ra_ref), (pb_ref, rb_ref)
    for g in range(ngroups):
        scores(0, g, *buf_a)

    def body(i, carry):
        step(2 * i, buf_a, buf_b)
        step(2 * i + 1, buf_b, buf_a)
        return carry

    lax.fori_loop(0, nk // 2 - 1, body, 0)
    step(nk - 2, buf_a, buf_b)
    step(nk - 1, buf_b, None)

    @pl.when(jnp.max(lag_ref[...]) > LAG_LIMIT)
    def _():
        _flash_loop(k_ref, vt_ref, q2_ref, m_ref, r_ref, acc_ref, pa_ref, sa_ref, sb_ref, ra_ref, rb_ref,
                    tq=tq, tk=tk, seq=seq, bias_fn=bias_fn)


def _flash_a_kernel(slope_ref, lam_ref, g_ref, qt_ref, k_ref, vt_ref, o_ref,
                    q2_ref, m_ref, a_ref, acc_ref, *bufs, tq, tk, seq, lam_init):
    _stack_pair_queries(q2_ref, qt_ref, 1, tq)
    *bufs, tiles_ref = bufs
    negc2 = -LOG2E * slope_ref[pl.program_id(1)]
    q0 = pl.program_id(2) * tq
    @pl.when(pl.program_id(2) == 0)
    def _():
        d0 = (lax.broadcasted_iota(jnp.int32, (tk, FLASH_COLS), 1)
              - lax.broadcasted_iota(jnp.int32, (tk, FLASH_COLS), 0)).astype(F32)
        tiles_ref[0] = d0 * negc2
        tiles_ref[1] = d0 * (-negc2)
        for n in range(tk // FLASH_COLS):
            tiles_ref[2 + n] = jnp.abs(d0 + float(n * FLASH_COLS)) * negc2

    qcol = lax.broadcasted_iota(jnp.int32, (1, FLASH_COLS), 1).astype(F32)

    def bias_fn(j, qcol0, rows=tk):
        delta = q0 + qcol0 - j * tk
        right, left = delta >= tk, delta <= -FLASH_COLS
        idx = jnp.where(right, 0, jnp.where(left, 1, 2 + jnp.clip(delta // FLASH_COLS, 0, tk // FLASH_COLS - 1)))
        shift = jnp.where(right | left, negc2 * jnp.abs(delta).astype(F32), 0.0)
        t = qcol + delta.astype(F32)
        nearest = jnp.maximum(jnp.maximum(t - float(rows - 1), -t), 0.0)
        return tiles_ref[idx, 0:rows, :], shift, negc2 * nearest

    _flash_attend(k_ref, vt_ref, q2_ref, m_ref, a_ref, acc_ref, *bufs,
                tq=tq, tk=tk, seq=seq, bias_fn=bias_fn)

    lp = lam_ref[...]
    lam = (jnp.exp(jnp.sum(lp[0:1] * lp[1:2], axis=1, keepdims=True))
           - jnp.exp(jnp.sum(lp[2:3] * lp[3:4], axis=1, keepdims=True)) + lam_init)
    vals, sums = slice(0, PAIR_W), slice(PAIR_W, PAIR_W + 1)
    o = (_read_cols(acc_ref, vals, 0, tq) / _read_cols(acc_ref, sums, 0, tq)
         - lam * (_read_cols(acc_ref, vals, tq, tq) / _read_cols(acc_ref, sums, tq, tq)))
    o = o * lax.rsqrt(jnp.mean(o * o, axis=0, keepdims=True) + EPS) * g_ref[...]
    o_ref[...] = (o * (1.0 - lam_init)).T.astype(o_ref.dtype)


def _flash_a(slopes, lamp, subg, qt, k, vt, lam_init, tq, tk):
    B, _, S = qt.shape
    tq, tk = min(tq, S), min(tk, S)
    M = 2 * tq
    return pl.pallas_call(
        functools.partial(_flash_a_kernel, tq=tq, tk=tk, seq=S, lam_init=lam_init),
        grid=(B, A_HEADS, S // tq),
        in_specs=[pl.BlockSpec(memory_space=pltpu.SMEM),
                  _const_spec((4, HEAD_DIM)), _const_spec((PAIR_W, 1)),
                  pl.BlockSpec((None, PAIR_W, tq), lambda b, h, i: (b, h, i)),
                  pl.BlockSpec((None, S, PAIR_W), lambda b, h, i: (b, 0, h)),
                  pl.BlockSpec((None, VT_ROWS, S), lambda b, h, i: (b, h, 0))],
        out_specs=pl.BlockSpec((None, tq, PAIR_W), lambda b, h, i: (b, i, h)),
        out_shape=jax.ShapeDtypeStruct((B, S, A_HEADS * PAIR_W), BF16),
        scratch_shapes=_flash_scratch(M, tk) + [pltpu.VMEM((2 + tk // FLASH_COLS, tk, FLASH_COLS), F32)],
        compiler_params=_cparams("parallel", "parallel", "arbitrary"),
        name="flash_diff",
    )(slopes, lamp, subg, qt, k, vt)


def _flash_d_kernel(qt_ref, k_ref, vt_ref, o_ref, q2_ref, m_ref, a_ref, acc_ref, *bufs, tq, tk, seq):
    nchunk = qt_ref.shape[0] // PAIR_W
    _stack_pair_queries(q2_ref, qt_ref, nchunk, tq)
    _flash_attend(k_ref, vt_ref, q2_ref, m_ref, a_ref, acc_ref, *bufs,
                tq=tq, tk=tk, seq=seq, bias_fn=None)
    sums = slice(PAIR_W, PAIR_W + 1)
    for c in range(nchunk):
        top, bot = (2 * c) * tq, (2 * c + 1) * tq
        ot = jnp.concatenate(
            [_read_cols(acc_ref, slice(0, HEAD_DIM), top, tq) / _read_cols(acc_ref, sums, top, tq),
             _read_cols(acc_ref, slice(HEAD_DIM, PAIR_W), bot, tq) / _read_cols(acc_ref, sums, bot, tq)],
            axis=0)
        o_ref[:, c * PAIR_W:(c + 1) * PAIR_W] = ot.T.astype(o_ref.dtype)


def _flash_d(qt, k, vt, tq, tk):
    B, qw, S = qt.shape
    tq, tk = min(tq, S), min(tk, S)
    M = 2 * (qw // PAIR_W) * tq
    return pl.pallas_call(
        functools.partial(_flash_d_kernel, tq=tq, tk=tk, seq=S),
        grid=(B, S // tq),
        in_specs=[pl.BlockSpec((None, qw, tq), lambda b, i: (b, 0, i)),
                  pl.BlockSpec((None, S, PAIR_W), lambda b, i: (b, 0, 0)),
                  pl.BlockSpec((None, VT_ROWS, S), lambda b, i: (b, 0, 0))],
        out_specs=pl.BlockSpec((None, tq, qw), lambda b, i: (b, i, 0)),
        out_shape=jax.ShapeDtypeStruct((B, S, qw), BF16),
        scratch_shapes=_flash_scratch(M, tk),
        compiler_params=_cparams("parallel", "parallel"),
        name="flash_gqa",
    )(qt, k, vt)


WIN_TQ = 128
WIN_SCORE_VREGS = 1536


def _win_kernel(tab_ref, qt_ref, k_ref, vt_ref, bias_ref, o_ref, *lse_refs, width, nblocks, unroll):
    row = lax.broadcasted_iota(jnp.int32, (PAIR_W, WIN_TQ), 0)
    pairs = qt_ref.shape[0] // PAIR_W

    def scores(pi, i):
        lanes = slice(pi * PAIR_W, (pi + 1) * PAIR_W)
        ws = pl.multiple_of(tab_ref[i, 0], 128)
        q0 = pl.multiple_of(i * WIN_TQ, WIN_TQ)
        qc = qt_ref[lanes, pl.ds(q0, WIN_TQ)].astype(F32)
        q2 = jnp.concatenate([jnp.where(row < HEAD_DIM, qc, 0.0),
                              jnp.where(row >= HEAD_DIM, qc, 0.0)], axis=1).astype(BF16)
        s = jnp.dot(k_ref[pl.ds(ws, width), lanes], q2, preferred_element_type=F32)
        return pi, lanes, ws, q0, s + bias_ref[tab_ref[i, 1], pi]

    def finish(lanes, q0, m, acc):
        l = acc[PAIR_W:PAIR_W + 1, :]
        rl = 1.0 / l
        ot = jnp.concatenate([acc[0:HEAD_DIM, 0:WIN_TQ] * rl[:, 0:WIN_TQ],
                              acc[HEAD_DIM:PAIR_W, WIN_TQ:] * rl[:, WIN_TQ:]], axis=0)
        o_ref[pl.ds(q0, WIN_TQ), lanes] = ot.T.astype(o_ref.dtype)
        if lse_refs:
            lse = m + jnp.log2(l)
            lt = jnp.concatenate([jnp.broadcast_to(lse[:, 0:WIN_TQ], (HEAD_DIM, WIN_TQ)),
                                  jnp.broadcast_to(lse[:, WIN_TQ:], (HEAD_DIM, WIN_TQ))], axis=0)
            lse_refs[0][pl.ds(q0, WIN_TQ), lanes] = lt.T

    def body(ib, carry):
        blocks = [scores(pi, ib * unroll + u) for pi in range(pairs) for u in range(unroll)]
        maxes, probs, accs = [], [], []

        def values(n):
            pi, ws = blocks[n][0], blocks[n][2]
            vt = vt_ref[pi * VT_ROWS:(pi + 1) * VT_ROWS, pl.ds(ws, width)]
            accs.append(jnp.dot(vt, probs[n], preferred_element_type=F32))

        for n, (*_, s) in enumerate(blocks):
            m = jnp.max(s, axis=0, keepdims=True)
            maxes.append(m)
            probs.append(jnp.exp2((s - m).astype(BF16)))
            if n > 0:
                values(n - 1)
        values(len(blocks) - 1)
        for (_, lanes, _, q0, _), m, acc in zip(blocks, maxes, accs):
            finish(lanes, q0, m, acc)
        return carry

    lax.fori_loop(0, nblocks // unroll, body, 0)


def _windowed(tab, qt, k, vt, bias, out_dtype, want_lse):
    N, _, L = qt.shape
    nvar, npair, width, _ = bias.shape
    nblocks = L // WIN_TQ
    in_flight = WIN_SCORE_VREGS // (width * 2 * WIN_TQ // VREG_ELEMS)
    in_flight = 1 << (in_flight.bit_length() - 1)
    assert nblocks % min(in_flight, nblocks) == 0
    unroll = min(in_flight, nblocks)
    pp = min(npair, in_flight // unroll)
    o_spec = pl.BlockSpec((None, L, pp * PAIR_W), lambda n, p: (n, 0, p))
    out_shape = [jax.ShapeDtypeStruct((N, L, npair * PAIR_W), out_dtype)]
    out_specs = [o_spec]
    if want_lse:
        out_shape.append(jax.ShapeDtypeStruct((N, L, npair * PAIR_W), F32))
        out_specs.append(o_spec)
    return pl.pallas_call(
        functools.partial(_win_kernel, width=width, nblocks=nblocks, unroll=unroll),
        grid=(N, npair // pp),
        in_specs=[pl.BlockSpec(memory_space=pltpu.SMEM),
                  pl.BlockSpec((None, pp * PAIR_W, L), lambda n, p: (n, p, 0)),
                  pl.BlockSpec((None, L, pp * PAIR_W), lambda n, p: (n, 0, p)),
                  pl.BlockSpec((None, pp * VT_ROWS, L), lambda n, p: (n, p, 0)),
                  pl.BlockSpec((nvar, pp, width, 2 * WIN_TQ), lambda n, p: (0, p, 0, 0))],
        out_specs=out_specs,
        out_shape=out_shape,
        compiler_params=_cparams("parallel", "parallel"),
        name=f"windowed_w{width}",
    )(tab, qt, k, vt, bias)


def _bmerge_kernel(o0_ref, l0_ref, o1_ref, l1_ref, o2_ref, l2_ref, y_ref, so1, sl1, so2, sl2, *, dils):
    T, width = y_ref.shape
    nchunk = width // LANES
    for (o_ref, l_ref, so, sl, dil) in ((o1_ref, l1_ref, so1, sl1, dils[1]), (o2_ref, l2_ref, so2, sl2, dils[2])):
        sub = T // dil
        for j in range(dil):
            for c in range(nchunk):
                so[c, pl.ds(j, sub, stride=dil), :] = o_ref[j, :, c * LANES:(c + 1) * LANES].astype(F32)
                sl[c, pl.ds(j, sub, stride=dil), :] = l_ref[j, :, c * LANES:(c + 1) * LANES]
    for c in range(nchunk):
        cols = slice(c * LANES, (c + 1) * LANES)
        o0, e0 = o0_ref[0, :, cols].astype(F32), l0_ref[0, :, cols]
        e1, e2 = sl1[c], sl2[c]
        mx = jnp.maximum(jnp.maximum(e0, e1), e2)
        w0, w1, w2 = jnp.exp2(e0 - mx), jnp.exp2(e1 - mx), jnp.exp2(e2 - mx)
        num = w0 * o0 + w1 * so1[c] + w2 * so2[c]
        y_ref[:, cols] = (num / (w0 + w1 + w2)).astype(y_ref.dtype)


def _bmerge(outs, S):
    B = outs[0][0].shape[0]
    dils = tuple(d for _, d in B_PATTERNS)
    T = min(512, S)
    args, in_specs = [], []
    for (o, e), dil in zip(outs, dils):
        spec = pl.BlockSpec((None, dil, T // dil, BRANCH_W), lambda b, t: (b, 0, t, 0))
        args += [o, e]
        in_specs += [spec, spec]
    return pl.pallas_call(
        functools.partial(_bmerge_kernel, dils=dils),
        grid=(B, S // T),
        in_specs=in_specs,
        out_specs=pl.BlockSpec((None, T, BRANCH_W), lambda b, t: (b, t, 0)),
        out_shape=jax.ShapeDtypeStruct((B, S, BRANCH_W), BF16),
        scratch_shapes=[pltpu.VMEM((BRANCH_W // LANES, T, LANES), F32)] * 4,
        compiler_params=_cparams("parallel", "parallel"),
        name="dilated_merge",
    )(*args)


def _merge_kernel(x_ref, nw_ref, ya_ref, yb_ref, yc_ref, yd_ref, wg_ref, bg_ref, wb_ref, wo_ref, o_ref):
    x = x_ref[...]
    h = _rms(x, nw_ref[...]).astype(BF16)
    merged = None
    for n, y_ref in enumerate((ya_ref, yb_ref, yc_ref, yd_ref)):
        cols = slice(n * D_MODEL, (n + 1) * D_MODEL)
        z = jnp.dot(h, wg_ref[:, cols], preferred_element_type=F32) + bg_ref[:, cols]
        gate = 1.0 / (1.0 + jnp.exp(-z))
        t = gate * jnp.dot(y_ref[...], wb_ref[n], preferred_element_type=F32)
        merged = t if merged is None else merged + t
    o_ref[...] = x + jnp.dot(merged.astype(BF16), wo_ref[...], preferred_element_type=F32)


def _merge(x2d, nw, ys, wg, bg, wb, wo, tm):
    Mrows, D = x2d.shape
    tm = min(tm, Mrows)
    row = lambda w: pl.BlockSpec((tm, w), lambda i: (i, 0))
    return pl.pallas_call(
        _merge_kernel,
        grid=(Mrows // tm,),
        in_specs=[row(D), _const_spec((1, D)), row(BRANCH_W), row(BRANCH_W), row(BRANCH_W), row(BRANCH_W),
                  _const_spec(wg.shape), _const_spec(bg.shape), _const_spec(wb.shape), _const_spec(wo.shape)],
        out_specs=row(D),
        out_shape=jax.ShapeDtypeStruct((Mrows, D), F32),
        compiler_params=_cparams("parallel"),
        name="gated_merge",
    )(x2d, nw, *ys, wg, bg, wb, wo)


FFN_CHUNK = 1024


def _ffn_kernel(x_ref, nw_ref, w1_ref, w2_ref, nf_ref, o_ref, *, final):
    x = x_ref[...]
    h = _rms(x, nw_ref[...]).astype(BF16)
    acc = x
    for c0 in range(0, D_FF, FFN_CHUNK):
        u = jnp.maximum(jnp.dot(h, w1_ref[:, c0:c0 + FFN_CHUNK], preferred_element_type=F32), 0.0)
        acc = acc + jnp.dot((u * u).astype(BF16), w2_ref[c0:c0 + FFN_CHUNK, :], preferred_element_type=F32)
    if final:
        acc = _rms(acc, nf_ref[...])
    o_ref[...] = acc


def _ffn(x2d, nw, w1, w2, nf, final, tm):
    Mrows, D = x2d.shape
    tm = min(tm, Mrows)
    row = pl.BlockSpec((tm, D), lambda i: (i, 0))
    return pl.pallas_call(
        functools.partial(_ffn_kernel, final=final),
        grid=(Mrows // tm,),
        in_specs=[row, _const_spec((1, D)), _const_spec(w1.shape), _const_spec(w2.shape), _const_spec((1, D))],
        out_specs=row,
        out_shape=jax.ShapeDtypeStruct((Mrows, D), F32),
        compiler_params=_cparams("parallel"),
        name="relu2_mlp",
    )(x2d, nw, w1, w2, nf)


def _alibi_slopes(n):
    return np.array([2.0 ** (-8.0 * (i + 1) / n) for i in range(n)], dtype=np.float32)


def _pair_tiles(t):
    nvar, nh, w, tq = t.shape
    t = t.reshape(nvar, nh // 2, 2, w, tq)
    return jnp.concatenate([t[:, :, 0], t[:, :, 1]], axis=-1)


def _dilated_tables(L, dil):
    width = 3 * WIN_TQ
    nb = L // WIN_TQ
    ws = np.clip((np.arange(nb) - 1) * WIN_TQ, 0, L - width)
    var = np.where(np.arange(nb) == 0, 0, np.where(np.arange(nb) == nb - 1, 2, 1))
    tab = np.stack([ws, var], axis=1).astype(np.int32)
    c = np.arange(width)[:, None]
    r = np.arange(WIN_TQ)[None, :]
    slopes = _alibi_slopes(B_HEADS)
    tiles = []
    for shift in (0, WIN_TQ, 2 * WIN_TQ):
        rel = c - shift - r
        valid = np.abs(rel) <= B_RADIUS
        bias = -slopes[:, None, None] * (np.abs(rel) * dil).astype(np.float32)[None] * np.float32(LOG2E)
        tiles.append(np.where(valid[None], bias, np.float32(NEG_INF)))
    return jnp.asarray(tab), _pair_tiles(jnp.asarray(np.stack(tiles).astype(np.float32)))


def _na_tables(S, rpb):
    rows = S // GRID_W
    kh = min(NA_KH, rows)
    wrows = 10
    nb = rows // 2
    r0s = 2 * np.arange(nb)
    wsr = np.clip(r0s - kh // 2, 0, rows - wrows)
    tab = np.stack([wsr * GRID_W, (r0s - wsr) // 2], axis=1).astype(np.int32)
    reps = [0, 2, 4, rows - 4, rows - 2]
    krl = np.arange(wrows)
    kc = np.arange(GRID_W)
    qrl = np.arange(2)
    qc = np.arange(GRID_W)
    qstart = np.clip(qc - NA_KW // 2, 0, GRID_W - NA_KW)
    col_ok = (kc[:, None] >= qstart[None, :]) & (kc[:, None] < qstart[None, :] + NA_KW)
    ridx, valid = [], []
    for r0 in reps:
        w0 = int(np.clip(r0 - kh // 2, 0, rows - wrows))
        r = (r0 + qrl)[None, :]
        rs = np.clip(r - kh // 2, 0, rows - kh)
        kr = (w0 + krl)[:, None]
        row_ok = (kr >= rs) & (kr < rs + kh)
        ridx.append(np.clip(kr - r + NA_KH - 1, 0, 2 * NA_KH - 2))
        valid.append(row_ok[:, None, :, None] & col_ok[None, :, None, :])
    ridx, valid = np.stack(ridx), np.stack(valid)
    nh = rpb.shape[0]
    span = GRID_W - NA_KW
    padded = jnp.pad(rpb.astype(F32), ((0, 0), (0, 0), (span, span)), mode="edge")
    period = 2 * GRID_W
    padded = jnp.pad(padded, ((0, 0), (0, 0), (0, 1)))
    shifted = jnp.tile(padded, (1, 1, GRID_W))[:, :, :GRID_W * (period - 1)]
    shifted = shifted.reshape(nh, -1, GRID_W, period - 1)
    by_col = jnp.swapaxes(shifted[:, :, :, GRID_W - 1:period - 1], -1, -2)
    slabs = jnp.take(by_col, jnp.asarray(ridx.reshape(-1)), axis=1)
    slabs = slabs.reshape(nh, len(reps), wrows, 2, GRID_W, GRID_W).transpose(1, 0, 2, 4, 3, 5)
    tiles = jnp.where(valid[:, None], slabs * LOG2E, NEG_INF)
    tiles = tiles.reshape(len(reps), nh, wrows * GRID_W, 2 * GRID_W)
    return jnp.asarray(tab), _pair_tiles(tiles)


def _rope_tables(S):
    n = HEAD_DIM // 4
    pos = np.arange(S)
    inv = ROPE_THETA ** (-np.arange(n, dtype=np.float32) / n)
    d = np.arange(HEAD_DIM)
    p = np.where((d < HEAD_DIM // 2)[:, None], (pos // GRID_W)[None, :], (pos % GRID_W)[None, :]).astype(np.float32)
    ang = p * inv[d % n][:, None]
    sign = np.where((d % (2 * n)) < n, -1.0, 1.0).astype(np.float32)[:, None]
    return jnp.asarray(np.cos(ang).astype(np.float32)), jnp.asarray((np.sin(ang) * sign).astype(np.float32))


def kernel(x, norm_mix, w_in, b_gate, diff_lambda, diff_subln, na_rpb, qk_norm, w_branch, w_out, norm_ffn,
           w_ff1, w_ff2, norm_final):
    B, S, D = x.shape
    depth = w_in.shape[0]
    rows = B * S

    ones_col = jnp.ones((1, 3 * BRANCH_W), F32)
    qscale_col = ones_col.at[:, 0:BRANCH_W].set(QK_SCALE * LOG2E)
    a_slopes = jnp.asarray(_alibi_slopes(A_HEADS))
    cos_t, sg_t = _rope_tables(S)
    b_tabs = [_dilated_tables(S // dil, dil) for _, dil in B_PATTERNS]
    d_perm = np.array([(g * 4 + c) * HEAD_DIM + d for c in range(4) for g in range(D_KV_HEADS)
                       for d in range(HEAD_DIM)])

    o_a, o_b, o_c, o_d, o_gate = 0, 1536, 6144, 7680, 8448
    for l in range(depth):
        w = w_in[l].astype(BF16)
        nw = norm_mix[l].reshape(1, D)
        lam_init = 0.8 - 0.6 * math.exp(-0.3 * l)

        aq, ak, av = _project(x, nw, w[:, o_a:o_a + 1536], qscale_col, 1, v_ones=True)
        y_a = _flash_a(a_slopes, diff_lambda[l], diff_subln[l].reshape(PAIR_W, 1),
                       aq[:, 0], ak[:, 0], av[:, 0], lam_init, FLASH_TQ_DIFF, FLASH_TK)

        b_outs = []
        for g, (_, dil) in enumerate(B_PATTERNS):
            lo = o_b + g * 1536
            bq, bk, bv = _project(x, nw, w[:, lo:lo + 1536], qscale_col, dil, v_ones=True)
            L = S // dil
            tab, bias = b_tabs[g]
            o, e = _windowed(tab, bq.reshape(B * dil, BRANCH_W, L), bk.reshape(B * dil, L, BRANCH_W),
                             bv.reshape(B * dil, -1, L), bias, BF16, True)
            b_outs.append((o.reshape(B, dil, L, BRANCH_W), e.reshape(B, dil, L, BRANCH_W)))
        y_b = _bmerge(b_outs, S)

        cq, ck, cv = _project(x, nw, w[:, o_c:o_c + 1536], qscale_col, 1, v_ones=True)
        c_tab, c_bias = _na_tables(S, na_rpb[l])
        (y_c,) = _windowed(c_tab, cq[:, 0], ck[:, 0], cv[:, 0], c_bias, BF16, False)

        wd = jnp.concatenate([w[:, o_d:o_d + 512][:, d_perm], w[:, o_d + 512:o_gate]], axis=1)
        dq, dk, dv = _project_d(x, nw, wd, qk_norm[l, 0].reshape(HEAD_DIM, 1), qk_norm[l, 1].reshape(HEAD_DIM, 1),
                                cos_t, sg_t)
        y_d = _flash_d(dq, dk, dv, FLASH_TQ_GQA, FLASH_TK)

        wb = w_branch[l].astype(BF16)
        wb = wb.at[3].set(wb[3][d_perm, :])
        ys = [y.reshape(rows, BRANCH_W) for y in (y_a, y_b, y_c, y_d)]
        x2d = _merge(x.reshape(rows, D), nw, ys, w[:, o_gate:], b_gate[l].reshape(1, -1), wb,
                     w_out[l].astype(BF16), DENSE_ROWS)

        x2d = _ffn(x2d, norm_ffn[l].reshape(1, D), w_ff1[l].astype(BF16), w_ff2[l].astype(BF16),
                   norm_final.reshape(1, D), l == depth - 1, DENSE_ROWS)
        x = x2d.reshape(B, S, D)
    return x
```
